```python
import functools
import jax, jax.numpy as jnp
from jax import lax
import numpy as np

D_MODEL = 2048
BATCH = 2
SEQ = 4096
DEPTH = 4
DEC_BATCH = 8
DEC_SEQ = 4
PAST_LEN = 16384
PAGE_SIZE = 128

BR_W = D_MODEL // 2
N_BRANCH = 4
POOL_WINDOWS = (2, 4, 8, 16)
POOL_GROUPS = 4
POOL_GROUP_W = BR_W // POOL_GROUPS
POOL_STATE = 15
GMLP_CHUNK = 128
GMLP_GROUPS = 8
GMLP_GROUP_W = BR_W // GMLP_GROUPS
GMLP_LN_EPS = 1e-5
RWKV_HEAD_DIM = 64
RWKV_HEADS = BR_W // RWKV_HEAD_DIM
RWKV_W_LORA = 64
RWKV_A_LORA = 64
RWKV_SHIFT_W = 3 * BR_W + RWKV_W_LORA + RWKV_A_LORA
RWKV_LN_EPS = 64e-5
ATTN_HEAD_DIM = 128
ATTN_HEADS = BR_W // ATTN_HEAD_DIM
IDX_HEADS = 16
IDX_DIM = 64
IDX_SCALE = (IDX_HEADS * IDX_DIM) ** -0.5
TOPK_MAX = 256
Q_BLOCK = 128
ROPE_THETA = 10000.0
NORM_EPS = 1e-6

SEGMENTS = (
    ('pool_u', BR_W), ('pool_gate', BR_W),
    ('gmlp_u', BR_W), ('gmlp_v', BR_W), ('gmlp_gate', BR_W),
    ('rwkv_shift', RWKV_SHIFT_W), ('rwkv_gate', BR_W),
    ('attn_q', BR_W), ('attn_k', BR_W), ('attn_v', BR_W),
    ('idx_q', IDX_HEADS * IDX_DIM), ('idx_k', IDX_DIM), ('idx_w', IDX_HEADS),
    ('attn_gate', BR_W), ('merge', N_BRANCH * D_MODEL),
)
IN_COLS = sum(w for _, w in SEGMENTS)

kernel_name = 'hybrid_pool_gmlp_rwkv7_dsa_step'

F32 = jnp.float32


def _split(p):
    out, o = {}, 0
    for name, w in SEGMENTS:
        out[name] = p[..., o:o + w]
        o += w
    return out


def _rmsnorm(x, g, eps=NORM_EPS):
    x32 = x.astype(F32)
    y = x32 * lax.rsqrt(jnp.mean(x32 * x32, axis=-1, keepdims=True) + eps)
    return (y * g.astype(F32)).astype(x.dtype)


def _rotary(x, pos):
    half = x.shape[-1] // 2
    freqs = ROPE_THETA ** (-jnp.arange(half, dtype=F32) / half)
    ang = pos.astype(F32)[:, None] * freqs[None, :]
    cos, sin = jnp.cos(ang)[:, None, :], jnp.sin(ang)[:, None, :]
    x32 = x.astype(F32)
    x1, x2 = x32[..., :half], x32[..., half:]
    return jnp.concatenate([x1 * cos - x2 * sin, x2 * cos + x1 * sin], axis=-1).astype(x.dtype)


def _pool_mixer(u, prev, pos0, w_group, scale):
    B, T, C = u.shape
    ext = jnp.concatenate([prev.astype(u.dtype), u], axis=1).astype(F32)
    cs = jnp.concatenate([jnp.zeros((B, 1, C), F32), jnp.cumsum(ext, axis=1)], axis=1)
    end = cs[:, POOL_STATE + 1:POOL_STATE + 1 + T]
    pos = pos0 + jnp.arange(T)
    outs = []
    for g, w in enumerate(POOL_WINDOWS):
        sl = slice(g * POOL_GROUP_W, (g + 1) * POOL_GROUP_W)
        start = cs[:, POOL_STATE + 1 - w:POOL_STATE + 1 - w + T, sl]
        cnt = jnp.minimum(w, pos + 1).astype(F32)[None, :, None]
        outs.append((end[..., sl] - start) / cnt - ext[:, POOL_STATE:, sl])
    pooled = jnp.stack(outs, axis=2)
    mixed = jnp.einsum('btgc,gcd->btgd', pooled, w_group.astype(F32)).reshape(B, T, C)
    return (mixed * scale.astype(F32)).astype(u.dtype), ext[:, -POOL_STATE:].astype(u.dtype)


def _gmlp_mixer(u, v, ln_g, ln_b, ws, bs):
    B, T, C = u.shape
    v32 = v.astype(F32)
    mu = jnp.mean(v32, axis=-1, keepdims=True)
    var = jnp.mean(jnp.square(v32 - mu), axis=-1, keepdims=True)
    vn = (v32 - mu) * lax.rsqrt(var + GMLP_LN_EPS) * ln_g.astype(F32) + ln_b.astype(F32)
    n_chunks = -(-T // GMLP_CHUNK)
    pad = n_chunks * GMLP_CHUNK - T
    vp = jnp.pad(vn, ((0, 0), (0, pad), (0, 0))).reshape(B, n_chunks, GMLP_CHUNK, GMLP_GROUPS, GMLP_GROUP_W)
    causal = jnp.tril(jnp.ones((GMLP_CHUNK, GMLP_CHUNK), dtype=bool))
    wm = jnp.where(causal[None], ws.astype(F32), 0.0)
    mixed = jnp.einsum('gpq,bnqgc->bnpgc', wm, vp) + bs.astype(F32).T[None, None, :, :, None]
    mixed = mixed.reshape(B, n_chunks * GMLP_CHUNK, C)[:, :T]
    return (u.astype(F32) * mixed).astype(u.dtype), vn.astype(u.dtype)


def _rwkv_mixer(zs, shift_prev, wkv_prev, mu, w0, w2, a0, a2, k_k, k_a, r_k, lnx_g, lnx_b):
    B, T, _ = zs.shape
    z = zs.astype(F32)
    prev = jnp.concatenate([shift_prev.astype(F32)[:, None], z[:, :-1]], axis=1)
    z = z + (prev - z) * mu.astype(F32)
    r = z[..., :BR_W]
    k = z[..., BR_W:2 * BR_W]
    v = z[..., 2 * BR_W:3 * BR_W]
    zw = z[..., 3 * BR_W:3 * BR_W + RWKV_W_LORA]
    za = z[..., 3 * BR_W + RWKV_W_LORA:]
    w = -jax.nn.softplus(-(w0.astype(F32) + jnp.tanh(zw) @ w2.astype(F32))) - 0.5
    decay = jnp.exp(-jnp.exp(w))
    a = jax.nn.sigmoid(a0.astype(F32) + za @ a2.astype(F32))
    hs = lambda t: t.reshape(B, T, RWKV_HEADS, RWKV_HEAD_DIM)
    kk = hs(k * k_k.astype(F32))
    kk = kk / jnp.maximum(jnp.sqrt(jnp.sum(kk * kk, axis=-1, keepdims=True)), 1e-12)
    k = k * (1.0 + (a - 1.0) * k_a.astype(F32))
    r_h, k_h, v_h, d_h, a_h = hs(r), hs(k), hs(v), hs(decay), hs(a)
    a_vec, b_vec = -kk, kk * a_h

    def step(S, inp):
        r_t, d_t, k_t, v_t, a_t, b_t = inp
        Sa = jnp.einsum('bhvk,bhk->bhv', S, a_t)
        S = S * d_t[:, :, None, :] + Sa[..., None] * b_t[:, :, None, :] + v_t[..., None] * k_t[:, :, None, :]
        return S, jnp.einsum('bhvk,bhk->bhv', S, r_t)

    tm = lambda t: jnp.moveaxis(t, 1, 0)
    S_fin, y = lax.scan(step, wkv_prev.astype(F32), (tm(r_h), tm(d_h), tm(k_h), tm(v_h), tm(a_vec), tm(b_vec)))
    y = jnp.moveaxis(y, 0, 1)
    m = jnp.mean(y, axis=-1, keepdims=True)
    var = jnp.mean(jnp.square(y - m), axis=-1, keepdims=True)
    y = ((y - m) * lax.rsqrt(var + RWKV_LN_EPS)).reshape(B, T, BR_W) * lnx_g.astype(F32) + lnx_b.astype(F32)
    bonus = jnp.sum(r_h * k_h * r_k.astype(F32), axis=-1, keepdims=True) * v_h
    y = y + bonus.reshape(B, T, BR_W)
    return y.astype(zs.dtype), zs[:, -1], S_fin.astype(zs.dtype)


def _gather_rows(rows, idx):
    return jax.vmap(lambda r, i: r[i])(rows, idx)


def _indexer_topk(qi, wi, ki, qpos):
    L = ki.shape[1]
    k_top = min(TOPK_MAX, L // 4)
    rel = jax.nn.relu(jnp.einsum('bqhd,bsd->bqhs', qi.astype(F32), ki.astype(F32)))
    score = jnp.einsum('bqh,bqhs->bqs', wi.astype(F32) * IDX_SCALE, rel)
    visible = jnp.arange(L)[None, None, :] <= qpos[None, :, None]
    score = jnp.where(visible, score, -jnp.inf)
    _, idx = lax.top_k(score, k_top)
    return idx, idx <= qpos[None, :, None]


def _sparse_attend(q, k_sel, v_sel, valid):
    s = jnp.einsum('bqhd,bqkhd->bqhk', q.astype(F32), k_sel.astype(F32)) * (ATTN_HEAD_DIM ** -0.5)
    s = jnp.where(valid[:, :, None, :], s, -jnp.inf)
    p = jax.nn.softmax(s, axis=-1)
    return jnp.einsum('bqhk,bqkhd->bqhd', p, v_sel.astype(F32)).astype(q.dtype)


def _attend_prompt(q, k, v, qi, wi, ki):
    B, T = q.shape[:2]
    nb = T // Q_BLOCK
    blk = lambda t: jnp.moveaxis(t.reshape((B, nb, Q_BLOCK) + t.shape[2:]), 1, 0)
    starts = jnp.arange(nb) * Q_BLOCK

    def one(args):
        q_b, qi_b, wi_b, s0 = args
        qpos = s0 + jnp.arange(Q_BLOCK)
        idx, valid = _indexer_topk(qi_b, wi_b, ki, qpos)
        return _sparse_attend(q_b, _gather_rows(k, idx), _gather_rows(v, idx), valid)

    out = lax.map(one, (blk(q), blk(qi), blk(wi), starts))
    return jnp.moveaxis(out, 0, 1).reshape(B, T, ATTN_HEADS, ATTN_HEAD_DIM)


def _attend_sample(q, k, v, qi, wi, ki, cache_k, cache_v, cache_kidx, page_table, layer):
    B, T = q.shape[:2]
    past = page_table.shape[1] * PAGE_SIZE
    past_ki = cache_kidx[page_table[:, :, None], layer, jnp.arange(PAGE_SIZE)[None, None, :]]
    ki_all = jnp.concatenate([past_ki.reshape(B, past, IDX_DIM).astype(ki.dtype), ki], axis=1)
    qpos = past + jnp.arange(T)
    idx, valid = _indexer_topk(qi, wi, ki_all, qpos)
    is_past = (idx < past)[..., None, None]
    pidx = jnp.minimum(idx, past - 1)
    phys = page_table[jnp.arange(B)[:, None, None], pidx // PAGE_SIZE]
    off = pidx % PAGE_SIZE
    nidx = jnp.clip(idx - past, 0, T - 1)
    k_sel = jnp.where(is_past, cache_k[phys, layer, off].astype(k.dtype), _gather_rows(k, nidx))
    v_sel = jnp.where(is_past, cache_v[phys, layer, off].astype(v.dtype), _gather_rows(v, nidx))
    return _sparse_attend(q, k_sel, v_sel, valid)


def _layer(x, p, pos0, pool_prev, shift_prev, wkv_prev, attend):
    B, T, _ = x.shape
    pos = pos0 + jnp.arange(T, dtype=jnp.int32)
    h = _rmsnorm(x, p['norm_g'])
    s = _split(h @ p['w_in'])
    pool_out, pool_state = _pool_mixer(s['pool_u'], pool_prev, pos0, p['pool_w'], p['pool_scale'])
    gmlp_out, gmlp_v = _gmlp_mixer(s['gmlp_u'], s['gmlp_v'], p['gmlp_ln_g'], p['gmlp_ln_b'], p['gmlp_ws'], p['gmlp_bs'])
    rwkv_out, shift_state, wkv_state = _rwkv_mixer(
        s['rwkv_shift'], shift_prev, wkv_prev, p['rwkv_mu'], p['rwkv_w0'], p['rwkv_w2'], p['rwkv_a0'],
        p['rwkv_a2'], p['rwkv_kk'], p['rwkv_ka'], p['rwkv_rk'], p['rwkv_lnx_g'], p['rwkv_lnx_b'])
    q = _rotary(_rmsnorm(s['attn_q'].reshape(B, T, ATTN_HEADS, ATTN_HEAD_DIM), p['attn_qn']), pos)
    k = _rotary(_rmsnorm(s['attn_k'].reshape(B, T, ATTN_HEADS, ATTN_HEAD_DIM), p['attn_kn']), pos)
    v = s['attn_v'].reshape(B, T, ATTN_HEADS, ATTN_HEAD_DIM)
    qi = _rotary(s['idx_q'].reshape(B, T, IDX_HEADS, IDX_DIM), pos)
    ki = _rotary(s['idx_k'][:, :, None, :], pos)[:, :, 0]
    attn_out = attend(q, k, v, qi, s['idx_w'], ki).reshape(B, T, BR_W)
    branches = jnp.stack([
        pool_out * jax.nn.silu(s['pool_gate']),
        gmlp_out * jax.nn.silu(s['gmlp_gate']),
        rwkv_out * jax.nn.silu(s['rwkv_gate']),
        attn_out * jax.nn.silu(s['attn_gate'])], axis=2)
    proj = jnp.einsum('btnc,ncd->btnd', branches, p['w_branch'])
    gates = jax.nn.sigmoid(s['merge'].reshape(B, T, N_BRANCH, D_MODEL))
    merged = jnp.sum(gates * proj, axis=2)
    y = x + merged @ p['w_out']
    return y, (k, v, ki, pool_state, shift_state, wkv_state, gmlp_v)


def setup_inputs(seed: int = 0) -> dict:
    key = jax.random.key(seed)
    ks = iter(jax.random.split(key, 40))
    nrm = lambda shape, scale: scale * jax.random.normal(next(ks), shape, F32)
    n_pages = PAST_LEN // PAGE_SIZE
    n_pool = (DEC_BATCH * n_pages * 5) // 4
    perm = jax.random.permutation(next(ks), n_pool)
    page_table = perm[:DEC_BATCH * n_pages].reshape(DEC_BATCH, n_pages).astype(jnp.int32)
    return {
        'x_prompt': nrm((BATCH, SEQ, D_MODEL), 1.0),
        'x_sample': nrm((DEC_BATCH, DEC_SEQ, D_MODEL), 1.0),
        'cache_k': nrm((n_pool, DEPTH, PAGE_SIZE, ATTN_HEADS, ATTN_HEAD_DIM), 1.0),
        'cache_v': nrm((n_pool, DEPTH, PAGE_SIZE, ATTN_HEADS, ATTN_HEAD_DIM), 1.0),
        'cache_kidx': nrm((n_pool, DEPTH, PAGE_SIZE, IDX_DIM), 1.0),
        'page_table': page_table,
        'state_pool': nrm((DEC_BATCH, DEPTH, POOL_STATE, BR_W), 1.0),
        'state_shift': nrm((DEC_BATCH, DEPTH, RWKV_SHIFT_W), 1.0),
        'state_wkv': nrm((DEC_BATCH, DEPTH, RWKV_HEADS, RWKV_HEAD_DIM, RWKV_HEAD_DIM), 0.3),
        'norm_g': 1.0 + nrm((DEPTH, D_MODEL), 0.02),
        'w_in': nrm((DEPTH, D_MODEL, IN_COLS), D_MODEL ** -0.5),
        'pool_w': nrm((DEPTH, POOL_GROUPS, POOL_GROUP_W, POOL_GROUP_W), POOL_GROUP_W ** -0.5),
        'pool_scale': 1.0 + nrm((DEPTH, BR_W), 0.02),
        'gmlp_ln_g': 1.0 + nrm((DEPTH, BR_W), 0.02),
        'gmlp_ln_b': nrm((DEPTH, BR_W), 0.02),
        'gmlp_ws': nrm((DEPTH, GMLP_GROUPS, GMLP_CHUNK, GMLP_CHUNK), GMLP_CHUNK ** -0.5),
        'gmlp_bs': 1.0 + nrm((DEPTH, GMLP_GROUPS, GMLP_CHUNK), 0.02),
        'rwkv_mu': jax.random.uniform(next(ks), (DEPTH, RWKV_SHIFT_W), F32),
        'rwkv_w0': -1.0 + nrm((DEPTH, BR_W), 0.5),
        'rwkv_w2': nrm((DEPTH, RWKV_W_LORA, BR_W), 0.1),
        'rwkv_a0': nrm((DEPTH, BR_W), 0.02),
        'rwkv_a2': nrm((DEPTH, RWKV_A_LORA, BR_W), 0.1),
        'rwkv_kk': 0.85 + nrm((DEPTH, BR_W), 0.02),
        'rwkv_ka': 1.0 + nrm((DEPTH, BR_W), 0.02),
        'rwkv_rk': nrm((DEPTH, RWKV_HEADS, RWKV_HEAD_DIM), 0.1),
        'rwkv_lnx_g': 1.0 + nrm((DEPTH, BR_W), 0.02),
        'rwkv_lnx_b': nrm((DEPTH, BR_W), 0.02),
        'attn_qn': 1.0 + nrm((DEPTH, ATTN_HEAD_DIM), 0.02),
        'attn_kn': 1.0 + nrm((DEPTH, ATTN_HEAD_DIM), 0.02),
        'w_branch': nrm((DEPTH, N_BRANCH, BR_W, D_MODEL), BR_W ** -0.5),
        'w_out': nrm((DEPTH, D_MODEL, D_MODEL), 0.5 * D_MODEL ** -0.5),
    }


def reference(x_prompt, x_sample, cache_k, cache_v, cache_kidx, page_table, state_pool, state_shift, state_wkv,
              norm_g, w_in, pool_w, pool_scale, gmlp_ln_g, gmlp_ln_b, gmlp_ws, gmlp_bs, rwkv_mu, rwkv_w0, rwkv_w2,
              rwkv_a0, rwkv_a2, rwkv_kk, rwkv_ka, rwkv_rk, rwkv_lnx_g, rwkv_lnx_b, attn_qn, attn_kn, w_branch, w_out):
    past = page_table.shape[1] * PAGE_SIZE
    bp = x_prompt.shape[0]
    xp, xs = x_prompt, x_sample
    outs_p, outs_s = [], []
    for l in range(DEPTH):
        p = {
            'norm_g': norm_g[l], 'w_in': w_in[l], 'pool_w': pool_w[l], 'pool_scale': pool_scale[l],
            'gmlp_ln_g': gmlp_ln_g[l], 'gmlp_ln_b': gmlp_ln_b[l], 'gmlp_ws': gmlp_ws[l], 'gmlp_bs': gmlp_bs[l],
            'rwkv_mu': rwkv_mu[l], 'rwkv_w0': rwkv_w0[l], 'rwkv_w2': rwkv_w2[l], 'rwkv_a0': rwkv_a0[l],
            'rwkv_a2': rwkv_a2[l], 'rwkv_kk': rwkv_kk[l], 'rwkv_ka': rwkv_ka[l], 'rwkv_rk': rwkv_rk[l],
            'rwkv_lnx_g': rwkv_lnx_g[l], 'rwkv_lnx_b': rwkv_lnx_b[l], 'attn_qn': attn_qn[l], 'attn_kn': attn_kn[l],
            'w_branch': w_branch[l], 'w_out': w_out[l],
        }
        xp, st_p = _layer(
            xp, p, 0,
            jnp.zeros((bp, POOL_STATE, BR_W), x_prompt.dtype),
            jnp.zeros((bp, RWKV_SHIFT_W), x_prompt.dtype),
            jnp.zeros((bp, RWKV_HEADS, RWKV_HEAD_DIM, RWKV_HEAD_DIM), x_prompt.dtype),
            _attend_prompt)
        attend_s = functools.partial(_attend_sample, cache_k=cache_k, cache_v=cache_v, cache_kidx=cache_kidx,
                                     page_table=page_table, layer=l)
        xs, st_s = _layer(xs, p, past, state_pool[:, l], state_shift[:, l], state_wkv[:, l], attend_s)
        outs_p.append(st_p)
        outs_s.append(st_s)
    stk = lambda outs, i: jnp.stack([o[i] for o in outs], axis=1)
    k_p, v_p, ki_p = stk(outs_p, 0), stk(outs_p, 1), stk(outs_p, 2)
    pool_p, shift_p, wkv_p = stk(outs_p, 3), stk(outs_p, 4), stk(outs_p, 5)
    k_s, v_s, ki_s = stk(outs_s, 0), stk(outs_s, 1), stk(outs_s, 2)
    pool_s, shift_s, wkv_s = stk(outs_s, 3), stk(outs_s, 4), stk(outs_s, 5)
    gmlp_v_s = stk(outs_s, 6)
    return (xp, xs, k_p, v_p, ki_p, k_s, v_s, ki_s, pool_p, pool_s, shift_p, shift_s, wkv_p, wkv_s, gmlp_v_s)
```

```python
import functools

import jax
import jax.numpy as jnp
from jax import lax
from jax.experimental import pallas as pl
from jax.experimental.pallas import tpu as pltpu

F32 = jnp.float32
BF16 = jnp.bfloat16

D_MODEL = 2048
DEPTH = 4
PAGE_SIZE = 128
BR_W = D_MODEL // 2
N_BRANCH = 4
POOL_WINDOWS = (2, 4, 8, 16)
POOL_GROUPS = 4
POOL_GROUP_W = BR_W // POOL_GROUPS
POOL_STATE = 15
GMLP_CHUNK = 128
GMLP_GROUPS = 8
GMLP_GROUP_W = BR_W // GMLP_GROUPS
GMLP_LN_EPS = 1e-5
RWKV_HEAD_DIM = 64
RWKV_HEADS = BR_W // RWKV_HEAD_DIM
RWKV_W_LORA = 64
RWKV_A_LORA = 64
RWKV_SHIFT_W = 3 * BR_W + RWKV_W_LORA + RWKV_A_LORA
RWKV_LN_EPS = 64e-5
ATTN_HEAD_DIM = 128
ATTN_HEADS = BR_W // ATTN_HEAD_DIM
IDX_HEADS = 16
IDX_DIM = 64
IDX_SCALE = (IDX_HEADS * IDX_DIM) ** -0.5
TOPK_MAX = 256
ROPE_THETA = 10000.0
NORM_EPS = 1e-6

SEG_SRC = (
    ('pool_u', BR_W), ('pool_gate', BR_W),
    ('gmlp_u', BR_W), ('gmlp_v', BR_W), ('gmlp_gate', BR_W),
    ('rwkv_shift', RWKV_SHIFT_W), ('rwkv_gate', BR_W),
    ('attn_q', BR_W), ('attn_k', BR_W), ('attn_v', BR_W),
    ('idx_q', IDX_HEADS * IDX_DIM), ('idx_k', IDX_DIM), ('idx_w', IDX_HEADS),
    ('attn_gate', BR_W), ('merge', N_BRANCH * D_MODEL),
)
SEG_DST_ORDER = ('pool_u', 'pool_gate', 'gmlp_u', 'gmlp_v', 'gmlp_gate', 'rwkv_gate',
                 'attn_q', 'attn_k', 'attn_v', 'idx_q', 'attn_gate', 'merge',
                 'rwkv_shift', 'idx_k', 'idx_w')
IN_COLS = sum(w for _, w in SEG_SRC)
IN_COLS_PAD = 23040
VMEM_LIMIT = 56 * 1024 * 1024


def _seg_offsets():
    src, o = {}, 0
    for name, w in SEG_SRC:
        src[name] = (o, w)
        o += w
    dst, o = {}, 0
    for name in SEG_DST_ORDER:
        dst[name] = o
        o += src[name][1]
    return src, dst


SRC_OFF, DST_OFF = _seg_offsets()


def _permute_w_in(w_in):
    parts = [w_in[..., SRC_OFF[n][0]:SRC_OFF[n][0] + SRC_OFF[n][1]] for n in SEG_DST_ORDER]
    parts.append(jnp.zeros(w_in.shape[:-1] + (IN_COLS_PAD - IN_COLS,), w_in.dtype))
    return jnp.concatenate(parts, axis=-1).astype(BF16)


def _seg(p, name):
    o = DST_OFF[name]
    return p[..., o:o + SRC_OFF[name][1]]


def _inproj_kernel(x_ref, g_ref, w_ref, o_ref, h_scr):
    @pl.when(pl.program_id(1) == 0)
    def _():
        x = x_ref[...]
        ms = jnp.mean(x * x, axis=-1, keepdims=True)
        h_scr[...] = (x * lax.rsqrt(ms + NORM_EPS) * g_ref[...]).astype(BF16)

    o_ref[...] = jnp.dot(h_scr[...], w_ref[...], preferred_element_type=F32)


def _inproj(x2d, g, w_bf16):
    m = x2d.shape[0]
    tm = min(m, 1024)
    tn = 512
    return pl.pallas_call(
        _inproj_kernel,
        grid=(m // tm, IN_COLS_PAD // tn),
        in_specs=[pl.BlockSpec((tm, D_MODEL), lambda i, j: (i, 0)),
                  pl.BlockSpec((1, D_MODEL), lambda i, j: (0, 0)),
                  pl.BlockSpec((D_MODEL, tn), lambda i, j: (0, j))],
        out_specs=pl.BlockSpec((tm, tn), lambda i, j: (i, j)),
        out_shape=jax.ShapeDtypeStruct((m, IN_COLS_PAD), F32),
        scratch_shapes=[pltpu.VMEM((tm, D_MODEL), BF16)],
        compiler_params=pltpu.CompilerParams(
            dimension_semantics=("parallel", "arbitrary"), vmem_limit_bytes=VMEM_LIMIT),
        name="inproj",
    )(x2d, g.reshape(1, D_MODEL), w_bf16)


def _merge_kernel(b0, b1, b2, b3, wb_ref, g0, g1, g2, g3, o_ref):
    acc = None
    for n, (b_ref, g_ref) in enumerate(((b0, g0), (b1, g1), (b2, g2), (b3, g3))):
        proj = jnp.dot(b_ref[...], wb_ref[n], preferred_element_type=F32)
        term = jax.nn.sigmoid(g_ref[...]) * proj
        acc = term if acc is None else acc + term
    o_ref[...] = acc.astype(BF16)


def _merge(branches, wb_bf16, p2d):
    m = p2d.shape[0]
    tm = min(m, 512)
    tn = 512
    goff = DST_OFF['merge'] // tn
    gstep = D_MODEL // tn
    bspec = pl.BlockSpec((tm, BR_W), lambda i, j: (i, 0))
    gspecs = [pl.BlockSpec((tm, tn), functools.partial(lambda i, j, n: (i, goff + n * gstep + j), n=n))
              for n in range(N_BRANCH)]
    return pl.pallas_call(
        _merge_kernel,
        grid=(m // tm, D_MODEL // tn),
        in_specs=[bspec, bspec, bspec, bspec,
                  pl.BlockSpec((N_BRANCH, BR_W, tn), lambda i, j: (0, 0, j))] + gspecs,
        out_specs=pl.BlockSpec((tm, tn), lambda i, j: (i, j)),
        out_shape=jax.ShapeDtypeStruct((m, D_MODEL), BF16),
        compiler_params=pltpu.CompilerParams(
            dimension_semantics=("parallel", "arbitrary"), vmem_limit_bytes=VMEM_LIMIT),
        name="merge",
    )(*branches, wb_bf16, p2d, p2d, p2d, p2d)


def _outproj_kernel(x_ref, m_ref, w_ref, o_ref):
    o_ref[...] = x_ref[...] + jnp.dot(m_ref[...], w_ref[...], preferred_element_type=F32)


def _outproj(x2d, merged, wo_bf16):
    m = x2d.shape[0]
    tm = min(m, 1024)
    tn = 512
    return pl.pallas_call(
        _outproj_kernel,
        grid=(m // tm, D_MODEL // tn),
        in_specs=[pl.BlockSpec((tm, tn), lambda i, j: (i, j)),
                  pl.BlockSpec((tm, D_MODEL), lambda i, j: (i, 0)),
                  pl.BlockSpec((D_MODEL, tn), lambda i, j: (0, j))],
        out_specs=pl.BlockSpec((tm, tn), lambda i, j: (i, j)),
        out_shape=jax.ShapeDtypeStruct((m, D_MODEL), F32),
        compiler_params=pltpu.CompilerParams(
            dimension_semantics=("parallel", "arbitrary"), vmem_limit_bytes=VMEM_LIMIT),
        name="outproj",
    )(x2d, merged, wo_bf16)


def _rmsnorm(x, g, eps=NORM_EPS):
    y = x * lax.rsqrt(jnp.mean(x * x, axis=-1, keepdims=True) + eps)
    return y * g


def _rotary(x, pos):
    half = x.shape[-1] // 2
    freqs = ROPE_THETA ** (-jnp.arange(half, dtype=F32) / half)
    ang = pos.astype(F32)[:, None] * freqs[None, :]
    cos, sin = jnp.cos(ang)[:, None, :], jnp.sin(ang)[:, None, :]
    x1, x2 = x[..., :half], x[..., half:]
    return jnp.concatenate([x1 * cos - x2 * sin, x2 * cos + x1 * sin], axis=-1)


def _pool_mixer(u, prev, pos0, w_group, scale):
    B, T, C = u.shape
    ext = jnp.concatenate([prev, u], axis=1)
    cs = jnp.concatenate([jnp.zeros((B, 1, C), F32), jnp.cumsum(ext, axis=1)], axis=1)
    end = cs[:, POOL_STATE + 1:POOL_STATE + 1 + T]
    pos = pos0 + jnp.arange(T)
    outs = []
    for g, w in enumerate(POOL_WINDOWS):
        sl = slice(g * POOL_GROUP_W, (g + 1) * POOL_GROUP_W)
        start = cs[:, POOL_STATE + 1 - w:POOL_STATE + 1 - w + T, sl]
        cnt = jnp.minimum(w, pos + 1).astype(F32)[None, :, None]
        outs.append((end[..., sl] - start) / cnt - ext[:, POOL_STATE:, sl])
    pooled = jnp.stack(outs, axis=2)
    mixed = jnp.einsum('btgc,gcd->btgd', pooled, w_group).reshape(B, T, C)
    return mixed * scale, ext[:, -POOL_STATE:]


def _gmlp_mixer(u, v, ln_g, ln_b, ws, bs):
    B, T, C = u.shape
    mu = jnp.mean(v, axis=-1, keepdims=True)
    var = jnp.mean(jnp.square(v - mu), axis=-1, keepdims=True)
    vn = (v - mu) * lax.rsqrt(var + GMLP_LN_EPS) * ln_g + ln_b
    n_chunks = -(-T // GMLP_CHUNK)
    pad = n_chunks * GMLP_CHUNK - T
    vp = jnp.pad(vn, ((0, 0), (0, pad), (0, 0))).reshape(B, n_chunks, GMLP_CHUNK, GMLP_GROUPS, GMLP_GROUP_W)
    causal = jnp.tril(jnp.ones((GMLP_CHUNK, GMLP_CHUNK), dtype=bool))
    wm = jnp.where(causal[None], ws, 0.0)
    mixed = jnp.einsum('gpq,bnqgc->bnpgc', wm, vp) + bs.T[None, None, :, :, None]
    mixed = mixed.reshape(B, n_chunks * GMLP_CHUNK, C)[:, :T]
    return u * mixed, vn


def _rwkv_mixer(zs, shift_prev, wkv_prev, mu, w0, w2, a0, a2, k_k, k_a, r_k, lnx_g, lnx_b):
    B, T, _ = zs.shape
    z = zs
    prev = jnp.concatenate([shift_prev[:, None], z[:, :-1]], axis=1)
    z = z + (prev - z) * mu
    r = z[..., :BR_W]
    k = z[..., BR_W:2 * BR_W]
    v = z[..., 2 * BR_W:3 * BR_W]
    zw = z[..., 3 * BR_W:3 * BR_W + RWKV_W_LORA]
    za = z[..., 3 * BR_W + RWKV_W_LORA:]
    w = -jax.nn.softplus(-(w0 + jnp.tanh(zw) @ w2)) - 0.5
    decay = jnp.exp(-jnp.exp(w))
    a = jax.nn.sigmoid(a0 + za @ a2)
    hs = lambda t: t.reshape(B, T, RWKV_HEADS, RWKV_HEAD_DIM)
    kk = hs(k * k_k)
    kk = kk / jnp.maximum(jnp.sqrt(jnp.sum(kk * kk, axis=-1, keepdims=True)), 1e-12)
    k = k * (1.0 + (a - 1.0) * k_a)
    r_h, k_h, v_h, d_h, a_h = hs(r), hs(k), hs(v), hs(decay), hs(a)
    a_vec, b_vec = -kk, kk * a_h

    def step(S, inp):
        r_t, d_t, k_t, v_t, a_t, b_t = inp
        Sa = jnp.einsum('bhvk,bhk->bhv', S, a_t)
        S = S * d_t[:, :, None, :] + Sa[..., None] * b_t[:, :, None, :] + v_t[..., None] * k_t[:, :, None, :]
        return S, jnp.einsum('bhvk,bhk->bhv', S, r_t)

    tm = lambda t: jnp.moveaxis(t, 1, 0)
    S_fin, y = lax.scan(step, wkv_prev, (tm(r_h), tm(d_h), tm(k_h), tm(v_h), tm(a_vec), tm(b_vec)))
    y = jnp.moveaxis(y, 0, 1)
    m = jnp.mean(y, axis=-1, keepdims=True)
    var = jnp.mean(jnp.square(y - m), axis=-1, keepdims=True)
    y = ((y - m) * lax.rsqrt(var + RWKV_LN_EPS)).reshape(B, T, BR_W) * lnx_g + lnx_b
    bonus = jnp.sum(r_h * k_h * r_k, axis=-1, keepdims=True) * v_h
    y = y + bonus.reshape(B, T, BR_W)
    return y, zs[:, -1], S_fin


def _gather_rows(rows, idx):
    return jax.vmap(lambda r, i: r[i])(rows, idx)


def _indexer_topk(qi, wi, ki, qpos):
    L = ki.shape[1]
    k_top = min(TOPK_MAX, L // 4)
    rel = jax.nn.relu(jnp.einsum('bqhd,bsd->bqhs', qi, ki))
    score = jnp.einsum('bqh,bqhs->bqs', wi * IDX_SCALE, rel)
    visible = jnp.arange(L)[None, None, :] <= qpos[None, :, None]
    score = jnp.where(visible, score, -jnp.inf)
    _, idx = lax.top_k(score, k_top)
    return idx, idx <= qpos[None, :, None]


def _sparse_attend(q, k_sel, v_sel, valid):
    s = jnp.einsum('bqhd,bqkhd->bqhk', q, k_sel) * (ATTN_HEAD_DIM ** -0.5)
    s = jnp.where(valid[:, :, None, :], s, -jnp.inf)
    p = jax.nn.softmax(s, axis=-1)
    return jnp.einsum('bqhk,bqkhd->bqhd', p, v_sel)


def _attend_prompt(q, k, v, qi, wi, ki):
    B, T = q.shape[:2]
    QB = 128
    nb = T // QB
    blk = lambda t: jnp.moveaxis(t.reshape((B, nb, QB) + t.shape[2:]), 1, 0)
    starts = jnp.arange(nb) * QB

    def one(args):
        q_b, qi_b, wi_b, s0 = args
        qpos = s0 + jnp.arange(QB)
        idx, valid = _indexer_topk(qi_b, wi_b, ki, qpos)
        return _sparse_attend(q_b, _gather_rows(k, idx), _gather_rows(v, idx), valid)

    out = lax.map(one, (blk(q), blk(qi), blk(wi), starts))
    return jnp.moveaxis(out, 0, 1).reshape(B, T, ATTN_HEADS, ATTN_HEAD_DIM)


def _attend_sample(q, k, v, qi, wi, ki, cache_k, cache_v, cache_kidx, page_table, layer):
    B, T = q.shape[:2]
    past = page_table.shape[1] * PAGE_SIZE
    past_ki = cache_kidx[page_table[:, :, None], layer, jnp.arange(PAGE_SIZE)[None, None, :]]
    ki_all = jnp.concatenate([past_ki.reshape(B, past, IDX_DIM), ki], axis=1)
    qpos = past + jnp.arange(T)
    idx, valid = _indexer_topk(qi, wi, ki_all, qpos)
    is_past = (idx < past)[..., None, None]
    pidx = jnp.minimum(idx, past - 1)
    phys = page_table[jnp.arange(B)[:, None, None], pidx // PAGE_SIZE]
    off = pidx % PAGE_SIZE
    nidx = jnp.clip(idx - past, 0, T - 1)
    k_sel = jnp.where(is_past, cache_k[phys, layer, off], _gather_rows(k, nidx))
    v_sel = jnp.where(is_past, cache_v[phys, layer, off], _gather_rows(v, nidx))
    return _sparse_attend(q, k_sel, v_sel, valid)


def _layer(x, lw, pos0, pool_prev, shift_prev, wkv_prev, attend):
    B, T, _ = x.shape
    x2d = x.reshape(B * T, D_MODEL)
    pos = pos0 + jnp.arange(T, dtype=jnp.int32)
    p2d = _inproj(x2d, lw['norm_g'], lw['w_in'])
    p = p2d.reshape(B, T, IN_COLS_PAD)
    s = lambda name: _seg(p, name)

    pool_out, pool_state = _pool_mixer(s('pool_u'), pool_prev, pos0, lw['pool_w'], lw['pool_scale'])
    gmlp_out, gmlp_v = _gmlp_mixer(s('gmlp_u'), s('gmlp_v'), lw['gmlp_ln_g'], lw['gmlp_ln_b'],
                                   lw['gmlp_ws'], lw['gmlp_bs'])
    rwkv_out, shift_state, wkv_state = _rwkv_mixer(
        s('rwkv_shift'), shift_prev, wkv_prev, lw['rwkv_mu'], lw['rwkv_w0'], lw['rwkv_w2'], lw['rwkv_a0'],
        lw['rwkv_a2'], lw['rwkv_kk'], lw['rwkv_ka'], lw['rwkv_rk'], lw['rwkv_lnx_g'], lw['rwkv_lnx_b'])
    hd = lambda t: t.reshape(B, T, ATTN_HEADS, ATTN_HEAD_DIM)
    q = _rotary(_rmsnorm(hd(s('attn_q')), lw['attn_qn']), pos)
    k = _rotary(_rmsnorm(hd(s('attn_k')), lw['attn_kn']), pos)
    v = hd(s('attn_v'))
    qi = _rotary(s('idx_q').reshape(B, T, IDX_HEADS, IDX_DIM), pos)
    ki = _rotary(s('idx_k')[:, :, None, :], pos)[:, :, 0]
    attn_out = attend(q, k, v, qi, s('idx_w'), ki).reshape(B, T, BR_W)

    flat = lambda t: t.reshape(B * T, BR_W).astype(BF16)
    branches = [flat(pool_out * jax.nn.silu(s('pool_gate'))),
                flat(gmlp_out * jax.nn.silu(s('gmlp_gate'))),
                flat(rwkv_out * jax.nn.silu(s('rwkv_gate'))),
                flat(attn_out * jax.nn.silu(s('attn_gate')))]
    merged = _merge(branches, lw['w_branch'], p2d)
    y = _outproj(x2d, merged, lw['w_out']).reshape(B, T, D_MODEL)
    return y, (k, v, ki, pool_state, shift_state, wkv_state, gmlp_v)


def kernel(x_prompt, x_sample, cache_k, cache_v, cache_kidx, page_table, state_pool, state_shift, state_wkv,
           norm_g, w_in, pool_w, pool_scale, gmlp_ln_g, gmlp_ln_b, gmlp_ws, gmlp_bs, rwkv_mu, rwkv_w0, rwkv_w2,
           rwkv_a0, rwkv_a2, rwkv_kk, rwkv_ka, rwkv_rk, rwkv_lnx_g, rwkv_lnx_b, attn_qn, attn_kn, w_branch, w_out):
    past = page_table.shape[1] * PAGE_SIZE
    bp = x_prompt.shape[0]
    w_in_p = _permute_w_in(w_in)
    w_branch_b = w_branch.astype(BF16)
    w_out_b = w_out.astype(BF16)
    xp, xs = x_prompt, x_sample
    outs_p, outs_s = [], []
    for l in range(DEPTH):
        lw = {
            'norm_g': norm_g[l], 'w_in': w_in_p[l], 'pool_w': pool_w[l], 'pool_scale': pool_scale[l],
            'gmlp_ln_g': gmlp_ln_g[l], 'gmlp_ln_b': gmlp_ln_b[l], 'gmlp_ws': gmlp_ws[l], 'gmlp_bs': gmlp_bs[l],
            'rwkv_mu': rwkv_mu[l], 'rwkv_w0': rwkv_w0[l], 'rwkv_w2': rwkv_w2[l], 'rwkv_a0': rwkv_a0[l],
            'rwkv_a2': rwkv_a2[l], 'rwkv_kk': rwkv_kk[l], 'rwkv_ka': rwkv_ka[l], 'rwkv_rk': rwkv_rk[l],
            'rwkv_lnx_g': rwkv_lnx_g[l], 'rwkv_lnx_b': rwkv_lnx_b[l], 'attn_qn': attn_qn[l],
            'attn_kn': attn_kn[l], 'w_branch': w_branch_b[l], 'w_out': w_out_b[l],
        }
        xp, st_p = _layer(
            xp, lw, 0,
            jnp.zeros((bp, POOL_STATE, BR_W), F32),
            jnp.zeros((bp, RWKV_SHIFT_W), F32),
            jnp.zeros((bp, RWKV_HEADS, RWKV_HEAD_DIM, RWKV_HEAD_DIM), F32),
            _attend_prompt)
        attend_s = functools.partial(_attend_sample, cache_k=cache_k, cache_v=cache_v, cache_kidx=cache_kidx,
                                     page_table=page_table, layer=l)
        xs, st_s = _layer(xs, lw, past, state_pool[:, l], state_shift[:, l], state_wkv[:, l], attend_s)
        outs_p.append(st_p)
        outs_s.append(st_s)
    stk = lambda outs, i: jnp.stack([o[i] for o in outs], axis=1)
    k_p, v_p, ki_p = stk(outs_p, 0), stk(outs_p, 1), stk(outs_p, 2)
    pool_p, shift_p, wkv_p = stk(outs_p, 3), stk(outs_p, 4), stk(outs_p, 5)
    k_s, v_s, ki_s = stk(outs_s, 0), stk(outs_s, 1), stk(outs_s, 2)
    pool_s, shift_s, wkv_s = stk(outs_s, 3), stk(outs_s, 4), stk(outs_s, 5)
    gmlp_v_s = stk(outs_s, 6)
    return (xp, xs, k_p, v_p, ki_p, k_s, v_s, ki_s, pool_p, pool_s, shift_p, shift_s, wkv_p, wkv_s, gmlp_v_s)
```

```python
import functools

import numpy as np
import jax
import jax.numpy as jnp
from jax import lax
from jax.experimental import pallas as pl
from jax.experimental.pallas import tpu as pltpu

F32 = jnp.float32
BF16 = jnp.bfloat16
I32 = jnp.int32

D_MODEL = 2048
DEPTH = 4
PAGE_SIZE = 128
BR_W = D_MODEL // 2
N_BRANCH = 4
POOL_WINDOWS = (2, 4, 8, 16)
POOL_GROUPS = 4
POOL_GROUP_W = BR_W // POOL_GROUPS
POOL_STATE = 15
POOL_HALO = 16
GMLP_CHUNK = 128
GMLP_GROUPS = 8
GMLP_GROUP_W = BR_W // GMLP_GROUPS
GMLP_LN_EPS = 1e-5
RWKV_HEAD_DIM = 64
RWKV_HEADS = BR_W // RWKV_HEAD_DIM
RWKV_PAIRS = RWKV_HEADS // 2
RWKV_W_LORA = 64
RWKV_A_LORA = 64
RWKV_SHIFT_W = 3 * BR_W + RWKV_W_LORA + RWKV_A_LORA
RWKV_LN_EPS = 64e-5
ATTN_HEAD_DIM = 128
ATTN_HEADS = BR_W // ATTN_HEAD_DIM
IDX_HEADS = 16
IDX_DIM = 64
IDX_SCALE = (IDX_HEADS * IDX_DIM) ** -0.5
TOPK_MAX = 256
Q_BLOCK = 128
ROPE_THETA = 10000.0
NORM_EPS = 1e-6
LANES = 128
SUBLANES = 8
BF16_ROWS = 16
INT_MIN = -2 ** 31
NEG_BIG = -1e30

SEG_SRC = (
    ('pool_u', BR_W), ('pool_gate', BR_W),
    ('gmlp_u', BR_W), ('gmlp_v', BR_W), ('gmlp_gate', BR_W),
    ('rwkv_shift', RWKV_SHIFT_W), ('rwkv_gate', BR_W),
    ('attn_q', BR_W), ('attn_k', BR_W), ('attn_v', BR_W),
    ('idx_q', IDX_HEADS * IDX_DIM), ('idx_k', IDX_DIM), ('idx_w', IDX_HEADS),
    ('attn_gate', BR_W), ('merge', N_BRANCH * D_MODEL),
)
SEG_DST_ORDER = ('pool_u', 'pool_gate', 'gmlp_u', 'gmlp_v', 'gmlp_gate', 'rwkv_gate',
                 'attn_q', 'attn_k', 'attn_v', 'idx_q', 'attn_gate', 'merge',
                 'rwkv_shift', 'idx_k', 'idx_w')
IN_COLS = sum(w for _, w in SEG_SRC)
IN_COLS_PAD = 23040
VMEM_LIMIT = 56 * 1024 * 1024


def _seg_offsets():
    src, o = {}, 0
    for name, w in SEG_SRC:
        src[name] = (o, w)
        o += w
    dst, o = {}, 0
    for name in SEG_DST_ORDER:
        dst[name] = o
        o += src[name][1]
    return src, dst


SRC_OFF, DST_OFF = _seg_offsets()
SMALL_W = 256
SMALL_BLK = (DST_OFF['rwkv_shift'] + 3 * BR_W) // SMALL_W
assert (DST_OFF['rwkv_shift'] + 3 * BR_W) % SMALL_W == 0
assert DST_OFF['idx_k'] == SMALL_BLK * SMALL_W + 128 and DST_OFF['idx_w'] == SMALL_BLK * SMALL_W + 192
assert RWKV_W_LORA + RWKV_A_LORA == LANES


def _cblk(name):
    assert DST_OFF[name] % BR_W == 0
    return DST_OFF[name] // BR_W


def _permute_w_in(w_in):
    parts = [w_in[..., SRC_OFF[n][0]:SRC_OFF[n][0] + SRC_OFF[n][1]] for n in SEG_DST_ORDER]
    parts.append(jnp.zeros(w_in.shape[:-1] + (IN_COLS_PAD - IN_COLS,), w_in.dtype))
    return jnp.concatenate(parts, axis=-1).astype(BF16)


def _params(sem):
    return pltpu.CompilerParams(dimension_semantics=sem, vmem_limit_bytes=VMEM_LIMIT)


def _branch_dtype(tt):
    return BF16 if tt % BF16_ROWS == 0 else F32


def _row_spec(tt, name):
    c = _cblk(name)
    return pl.BlockSpec((1, tt, BR_W), lambda i, t: (i, t, c))


def _silu(x):
    return x * jax.nn.sigmoid(x)


def _head_ones():
    r = np.arange(2 * LANES)[:, None] % LANES
    c = np.arange(LANES)[None, :]
    return jnp.asarray((r // RWKV_HEAD_DIM) == (c // RWKV_HEAD_DIM), dtype=BF16)


def _seg_sum(x, ones2):
    hi = x.astype(BF16)
    lo = (x - hi.astype(F32)).astype(BF16)
    return jnp.dot(jnp.concatenate([hi, lo], axis=1), ones2, preferred_element_type=F32)


def _put_row(dst, row, i):
    sub = lax.broadcasted_iota(I32, dst.shape, 0)
    return jnp.where(sub == i, jnp.broadcast_to(row, dst.shape), dst)


def _inproj_kernel(x_ref, g_ref, w_ref, o_ref, h_scr):
    @pl.when(pl.program_id(1) == 0)
    def _():
        x = x_ref[...]
        ms = jnp.mean(x * x, axis=-1, keepdims=True)
        h_scr[...] = (x * lax.rsqrt(ms + NORM_EPS) * g_ref[...]).astype(BF16)

    o_ref[...] = jnp.dot(h_scr[...], w_ref[...], preferred_element_type=F32)


def _inproj(x2d, g, w_bf16):
    m = x2d.shape[0]
    tm = min(m, 1024)
    tn = 512
    return pl.pallas_call(
        _inproj_kernel,
        grid=(m // tm, IN_COLS_PAD // tn),
        in_specs=[pl.BlockSpec((tm, D_MODEL), lambda i, j: (i, 0)),
                  pl.BlockSpec((1, D_MODEL), lambda i, j: (0, 0)),
                  pl.BlockSpec((D_MODEL, tn), lambda i, j: (0, j))],
        out_specs=pl.BlockSpec((tm, tn), lambda i, j: (i, j)),
        out_shape=jax.ShapeDtypeStruct((m, IN_COLS_PAD), F32),
        scratch_shapes=[pltpu.VMEM((tm, D_MODEL), BF16)],
        compiler_params=_params(("parallel", "arbitrary")),
        name="inproj",
    )(x2d, g.reshape(1, D_MODEL), w_bf16)


def _merge_kernel(b0, b1, b2, b3, wb_ref, g0, g1, g2, g3, o_ref):
    acc = None
    for n, (b_ref, g_ref) in enumerate(((b0, g0), (b1, g1), (b2, g2), (b3, g3))):
        proj = jnp.dot(b_ref[...], wb_ref[n], preferred_element_type=F32)
        term = jax.nn.sigmoid(g_ref[...]) * proj
        acc = term if acc is None else acc + term
    o_ref[...] = acc.astype(BF16)


def _merge(branches, wb_bf16, p2d):
    m = p2d.shape[0]
    tm = min(m, 512)
    tn = 512
    goff = DST_OFF['merge'] // tn
    gstep = D_MODEL // tn
    bspec = pl.BlockSpec((tm, BR_W), lambda i, j: (i, 0))

    def gate_map(n):
        return lambda i, j: (i, goff + n * gstep + j)

    gspecs = [pl.BlockSpec((tm, tn), gate_map(n)) for n in range(N_BRANCH)]
    return pl.pallas_call(
        _merge_kernel,
        grid=(m // tm, D_MODEL // tn),
        in_specs=[bspec, bspec, bspec, bspec,
                  pl.BlockSpec((N_BRANCH, BR_W, tn), lambda i, j: (0, 0, j))] + gspecs,
        out_specs=pl.BlockSpec((tm, tn), lambda i, j: (i, j)),
        out_shape=jax.ShapeDtypeStruct((m, D_MODEL), BF16),
        compiler_params=_params(("parallel", "arbitrary")),
        name="merge",
    )(*branches, wb_bf16, p2d, p2d, p2d, p2d)


def _outproj_kernel(x_ref, m_ref, w_ref, o_ref):
    o_ref[...] = x_ref[...] + jnp.dot(m_ref[...], w_ref[...], preferred_element_type=F32)


def _outproj(x2d, merged, wo_bf16):
    m = x2d.shape[0]
    tm = min(m, 1024)
    tn = 512
    return pl.pallas_call(
        _outproj_kernel,
        grid=(m // tm, D_MODEL // tn),
        in_specs=[pl.BlockSpec((tm, tn), lambda i, j: (i, j)),
                  pl.BlockSpec((tm, D_MODEL), lambda i, j: (i, 0)),
                  pl.BlockSpec((D_MODEL, tn), lambda i, j: (0, j))],
        out_specs=pl.BlockSpec((tm, tn), lambda i, j: (i, j)),
        out_shape=jax.ShapeDtypeStruct((m, D_MODEL), F32),
        compiler_params=_params(("parallel", "arbitrary")),
        name="outproj",
    )(x2d, merged, wo_bf16)


def _pool_kernel(u_ref, gate_ref, prev_ref, w_ref, scale_ref, o_ref, st_ref, ext, *, tt, pos0):
    t = pl.program_id(1)
    nt = pl.num_programs(1)

    @pl.when(t == 0)
    def _():
        ext[0:1, :] = jnp.zeros((1, BR_W), F32)
        ext[1:POOL_HALO, :] = prev_ref[0]

    @pl.when(t > 0)
    def _():
        ext[0:POOL_HALO, :] = ext[tt:tt + POOL_HALO, :]

    ext[POOL_HALO:POOL_HALO + tt, :] = u_ref[0]
    pos = pos0 + t * tt + lax.broadcasted_iota(I32, (tt, POOL_GROUP_W), 0)
    for g, w in enumerate(POOL_WINDOWS):
        sl = slice(g * POOL_GROUP_W, (g + 1) * POOL_GROUP_W)
        cur = ext[POOL_HALO:POOL_HALO + tt, sl]
        acc = cur
        for i in range(1, w):
            acc = acc + ext[POOL_HALO - i:POOL_HALO - i + tt, sl]
        cnt = jnp.minimum(w, pos + 1).astype(F32)
        pooled = acc / cnt - cur
        mixed = jnp.dot(pooled.astype(BF16), w_ref[g], preferred_element_type=F32)
        o_ref[0, :, sl] = (mixed * scale_ref[:, sl] * _silu(gate_ref[0, :, sl])).astype(o_ref.dtype)

    @pl.when(t == nt - 1)
    def _():
        st_ref[0] = ext[tt + 1:tt + POOL_HALO, :]


def _pool(p3d, prev, pool_w_bf16, scale, pos0):
    b, t_len, _ = p3d.shape
    tt = min(t_len, 512)
    return pl.pallas_call(
        functools.partial(_pool_kernel, tt=tt, pos0=pos0),
        grid=(b, t_len // tt),
        in_specs=[_row_spec(tt, 'pool_u'), _row_spec(tt, 'pool_gate'),
                  pl.BlockSpec((1, POOL_STATE, BR_W), lambda i, t: (i, 0, 0)),
                  pl.BlockSpec((POOL_GROUPS, POOL_GROUP_W, POOL_GROUP_W), lambda i, t: (0, 0, 0)),
                  pl.BlockSpec((1, BR_W), lambda i, t: (0, 0))],
        out_specs=[pl.BlockSpec((1, tt, BR_W), lambda i, t: (i, t, 0)),
                   pl.BlockSpec((1, POOL_STATE, BR_W), lambda i, t: (i, 0, 0))],
        out_shape=[jax.ShapeDtypeStruct((b, t_len, BR_W), _branch_dtype(tt)),
                   jax.ShapeDtypeStruct((b, POOL_STATE, BR_W), F32)],
        scratch_shapes=[pltpu.VMEM((POOL_HALO + tt, BR_W), F32)],
        compiler_params=_params(("parallel", "arbitrary")),
        name="pool",
    )(p3d, p3d, prev, pool_w_bf16, scale.reshape(1, BR_W))


def _gmlp_kernel(u_ref, v_ref, gate_ref, lng_ref, lnb_ref, wm_ref, bsb_ref, o_ref, *rest, tt, emit_vn):
    if emit_vn:
        vn_ref = rest[0]
        rest = rest[1:]
    v = v_ref[0]
    mu = jnp.mean(v, axis=-1, keepdims=True)
    var = jnp.mean(jnp.square(v - mu), axis=-1, keepdims=True)
    vn = (v - mu) * lax.rsqrt(var + GMLP_LN_EPS) * lng_ref[...] + lnb_ref[...]
    if emit_vn:
        vn_ref[0] = vn
    if tt % GMLP_CHUNK == 0:
        rows = GMLP_CHUNK
        n_chunks = tt // GMLP_CHUNK
        vnb = vn.astype(BF16)
    else:
        pad = rest[0]
        rows = tt
        n_chunks = 1
        pad[...] = jnp.zeros_like(pad)
        pad[0:tt, :] = vn
        vnb = pad[...].astype(BF16)
    for c in range(n_chunks):
        for g in range(GMLP_GROUPS):
            sl = slice(g * GMLP_GROUP_W, (g + 1) * GMLP_GROUP_W)
            rs = slice(c * rows, (c + 1) * rows)
            vc = vnb[c * GMLP_CHUNK:(c + 1) * GMLP_CHUNK, sl]
            mixed = jnp.dot(wm_ref[g], vc, preferred_element_type=F32) + bsb_ref[g]
            o_ref[0, rs, sl] = (u_ref[0, rs, sl] * mixed[0:rows] * _silu(gate_ref[0, rs, sl])).astype(o_ref.dtype)


def _gmlp(p3d, ln_g, ln_b, wm_bf16, bsb, emit_vn):
    b, t_len, _ = p3d.shape
    tt = min(t_len, 512)
    assert tt % GMLP_CHUNK == 0 or (tt == t_len and tt < GMLP_CHUNK)
    out_specs = [pl.BlockSpec((1, tt, BR_W), lambda i, t: (i, t, 0))]
    out_shape = [jax.ShapeDtypeStruct((b, t_len, BR_W), _branch_dtype(tt))]
    if emit_vn:
        out_specs.append(pl.BlockSpec((1, tt, BR_W), lambda i, t: (i, t, 0)))
        out_shape.append(jax.ShapeDtypeStruct((b, t_len, BR_W), F32))
    scratch = [] if tt % GMLP_CHUNK == 0 else [pltpu.VMEM((GMLP_CHUNK, BR_W), F32)]
    res = pl.pallas_call(
        functools.partial(_gmlp_kernel, tt=tt, emit_vn=emit_vn),
        grid=(b, t_len // tt),
        in_specs=[_row_spec(tt, 'gmlp_u'), _row_spec(tt, 'gmlp_v'), _row_spec(tt, 'gmlp_gate'),
                  pl.BlockSpec((1, BR_W), lambda i, t: (0, 0)),
                  pl.BlockSpec((1, BR_W), lambda i, t: (0, 0)),
                  pl.BlockSpec((GMLP_GROUPS, GMLP_CHUNK, GMLP_CHUNK), lambda i, t: (0, 0, 0)),
                  pl.BlockSpec((GMLP_GROUPS, GMLP_CHUNK, GMLP_GROUP_W), lambda i, t: (0, 0, 0))],
        out_specs=out_specs,
        out_shape=out_shape,
        scratch_shapes=scratch,
        compiler_params=_params(("parallel", "parallel")),
        name="gmlp",
    )(p3d, p3d, p3d, ln_g.reshape(1, BR_W), ln_b.reshape(1, BR_W), wm_bf16, bsb)
    return (res[0], res[1]) if emit_vn else (res[0], None)


def _rwkv_prep_kernel(r_ref, k_ref, v_ref, sm_ref, prev_ref, mu_ref, w0_ref, a0_ref, wl_ref, al_ref,
                      kk_ref, ka_ref, rk_ref, ones_ref,
                      ro_ref, do_ref, ko_ref, vo_ref, ao_ref, bo_ref, bonus_ref, st_ref,
                      buf, *, tt):
    t = pl.program_id(1)
    nt = pl.num_programs(1)

    @pl.when(t == 0)
    def _():
        buf[7:8, :] = prev_ref[0]

    @pl.when(t > 0)
    def _():
        buf[7:8, :] = buf[7 + tt:8 + tt, :]

    buf[8:8 + tt, 0:BR_W] = r_ref[0]
    buf[8:8 + tt, BR_W:2 * BR_W] = k_ref[0]
    buf[8:8 + tt, 2 * BR_W:3 * BR_W] = v_ref[0]
    buf[8:8 + tt, 3 * BR_W:RWKV_SHIFT_W] = sm_ref[0, :, 0:LANES]

    @pl.when(t == nt - 1)
    def _():
        st_ref[0] = buf[7 + tt:8 + tt, :]

    ones2 = ones_ref[...]

    def mixed(lo, hi):
        cur = buf[8:8 + tt, lo:hi]
        prv = buf[7:7 + tt, lo:hi]
        return cur + (prv - cur) * mu_ref[:, lo:hi]

    zs = mixed(3 * BR_W, RWKV_SHIFT_W)
    w_pre = w0_ref[...] + jnp.dot(jnp.tanh(zs).astype(BF16), wl_ref[...], preferred_element_type=F32)
    w = -jax.nn.softplus(-w_pre) - 0.5
    do_ref[0] = jnp.exp(-jnp.exp(w))
    a = jax.nn.sigmoid(a0_ref[...] + jnp.dot(zs.astype(BF16), al_ref[...], preferred_element_type=F32))
    r = mixed(0, BR_W)
    k = mixed(BR_W, 2 * BR_W)
    v = mixed(2 * BR_W, 3 * BR_W)
    ro_ref[0] = r
    vo_ref[0] = v
    k2 = k * (1.0 + (a - 1.0) * ka_ref[...])
    ko_ref[0] = k2
    kk = k * kk_ref[...]
    rkk = r * k2 * rk_ref[...]
    for j in range(RWKV_PAIRS):
        sl = slice(j * LANES, (j + 1) * LANES)
        kkj = kk[:, sl]
        nrm = jnp.sqrt(_seg_sum(kkj * kkj, ones2))
        kkn = kkj / jnp.maximum(nrm, 1e-12)
        ao_ref[0, :, sl] = -kkn
        bo_ref[0, :, sl] = kkn * a[:, sl]
        bonus_ref[0, :, sl] = _seg_sum(rkk[:, sl], ones2) * v[:, sl]


def _rwkv_prep(p3d, shift_prev, lw):
    b, t_len, _ = p3d.shape
    tt = min(t_len, 256)
    base = _cblk('rwkv_shift')

    def row(c):
        return pl.BlockSpec((1, tt, BR_W), lambda i, t: (i, t, c))

    vec = pl.BlockSpec((1, BR_W), lambda i, t: (0, 0))
    out_row = pl.BlockSpec((1, tt, BR_W), lambda i, t: (i, t, 0))
    big = jax.ShapeDtypeStruct((b, t_len, BR_W), F32)
    outs = pl.pallas_call(
        functools.partial(_rwkv_prep_kernel, tt=tt),
        grid=(b, t_len // tt),
        in_specs=[row(base), row(base + 1), row(base + 2),
                  pl.BlockSpec((1, tt, SMALL_W), lambda i, t: (i, t, SMALL_BLK)),
                  pl.BlockSpec((1, 1, RWKV_SHIFT_W), lambda i, t: (i, 0, 0)),
                  pl.BlockSpec((1, RWKV_SHIFT_W), lambda i, t: (0, 0)),
                  vec, vec,
                  pl.BlockSpec((LANES, BR_W), lambda i, t: (0, 0)),
                  pl.BlockSpec((LANES, BR_W), lambda i, t: (0, 0)),
                  vec, vec, vec,
                  pl.BlockSpec((2 * LANES, LANES), lambda i, t: (0, 0))],
        out_specs=[out_row] * 7 + [pl.BlockSpec((1, 1, RWKV_SHIFT_W), lambda i, t: (i, 0, 0))],
        out_shape=[big] * 7 + [jax.ShapeDtypeStruct((b, 1, RWKV_SHIFT_W), F32)],
        scratch_shapes=[pltpu.VMEM((8 + tt, RWKV_SHIFT_W), F32)],
        compiler_params=_params(("parallel", "arbitrary")),
        name="rwkv_prep",
    )(p3d, p3d, p3d, p3d, shift_prev.reshape(b, 1, RWKV_SHIFT_W), lw['rwkv_mu'].reshape(1, RWKV_SHIFT_W),
      lw['rwkv_w0'].reshape(1, BR_W), lw['rwkv_a0'].reshape(1, BR_W), lw['rwkv_wl'], lw['rwkv_al'],
      lw['rwkv_kk'].reshape(1, BR_W), lw['rwkv_ka'].reshape(1, BR_W), lw['rwkv_rk'].reshape(1, BR_W),
      _head_ones())
    return outs[:7], outs[7].reshape(b, RWKV_SHIFT_W)


def _rwkv_scan_kernel(r_ref, d_ref, k_ref, v_ref, a_ref, b_ref, s0_ref, ones_ref, y_ref, sf_ref,
                      s_scr, dm_scr, *, nb, tc):
    t = pl.program_id(1)
    nt = pl.num_programs(1)
    rows = nb * RWKV_PAIRS * RWKV_HEAD_DIM
    grp = min(SUBLANES, tc)

    @pl.when(t == 0)
    def _():
        s_scr[...] = s0_ref[...].reshape(rows, LANES)
        ri = lax.broadcasted_iota(I32, (rows, LANES), 0) % RWKV_HEAD_DIM
        ci = lax.broadcasted_iota(I32, (rows, LANES), 1) % RWKV_HEAD_DIM
        dm_scr[...] = jnp.where(ri == ci, 1.0, 0.0)

    ones2 = ones_ref[...]

    def spread(tile, i):
        pieces = []
        for bb in range(nb):
            for p in range(RWKV_PAIRS):
                pieces.append(jnp.broadcast_to(tile[bb, i:i + 1, p * LANES:(p + 1) * LANES],
                                               (RWKV_HEAD_DIM, LANES)))
        return jnp.concatenate(pieces, axis=0)

    def group(gi, carry):
        g0 = pl.multiple_of(gi * grp, grp)
        rt = r_ref[:, pl.ds(g0, grp), :]
        dt = d_ref[:, pl.ds(g0, grp), :]
        kt = k_ref[:, pl.ds(g0, grp), :]
        vt = v_ref[:, pl.ds(g0, grp), :]
        at = a_ref[:, pl.ds(g0, grp), :]
        bt = b_ref[:, pl.ds(g0, grp), :]
        ys = [jnp.zeros((grp, LANES), F32) for _ in range(nb * RWKV_PAIRS)]
        for i in range(grp):
            dm = dm_scr[...]
            s = s_scr[...]
            sa = _seg_sum(s * spread(at, i), ones2)
            vcol = _seg_sum(dm * spread(vt, i), ones2)
            s = s * spread(dt, i) + sa * spread(bt, i) + vcol * spread(kt, i)
            s_scr[...] = s
            yd = _seg_sum(s * spread(rt, i), ones2) * dm
            for q in range(nb * RWKV_PAIRS):
                yrow = jnp.sum(yd[q * RWKV_HEAD_DIM:(q + 1) * RWKV_HEAD_DIM], axis=0, keepdims=True)
                ys[q] = _put_row(ys[q], yrow, i)
        for bb in range(nb):
            for p in range(RWKV_PAIRS):
                y_ref[bb, p, pl.ds(g0, grp), :] = ys[bb * RWKV_PAIRS + p]
        return carry

    lax.fori_loop(0, tc // grp, group, 0)

    @pl.when(t == nt - 1)
    def _():
        sf_ref[...] = s_scr[...].reshape(nb, RWKV_PAIRS, RWKV_HEAD_DIM, LANES)


def _pair_state(s):
    b = s.shape[0]
    s = s.reshape(b, RWKV_PAIRS, 2, RWKV_HEAD_DIM, RWKV_HEAD_DIM)
    return jnp.transpose(s, (0, 1, 3, 2, 4)).reshape(b, RWKV_PAIRS, RWKV_HEAD_DIM, LANES)


def _unpair_state(s):
    b = s.shape[0]
    s = s.reshape(b, RWKV_PAIRS, RWKV_HEAD_DIM, 2, RWKV_HEAD_DIM)
    return jnp.transpose(s, (0, 1, 3, 2, 4)).reshape(b, RWKV_HEADS, RWKV_HEAD_DIM, RWKV_HEAD_DIM)


def _rwkv_scan(seqs, wkv_prev):
    r, d, k, v, a, bv = seqs
    b, t_len, _ = r.shape
    nb = 2
    tc = min(t_len, 256)
    row = pl.BlockSpec((nb, tc, BR_W), lambda i, t: (i, t, 0))
    st = pl.BlockSpec((nb, RWKV_PAIRS, RWKV_HEAD_DIM, LANES), lambda i, t: (i, 0, 0, 0))
    rows = nb * RWKV_PAIRS * RWKV_HEAD_DIM
    y, sf = pl.pallas_call(
        functools.partial(_rwkv_scan_kernel, nb=nb, tc=tc),
        grid=(b // nb, t_len // tc),
        in_specs=[row] * 6 + [st, pl.BlockSpec((2 * LANES, LANES), lambda i, t: (0, 0))],
        out_specs=[pl.BlockSpec((nb, RWKV_PAIRS, tc, LANES), lambda i, t: (i, 0, t, 0)), st],
        out_shape=[jax.ShapeDtypeStruct((b, RWKV_PAIRS, t_len, LANES), F32),
                   jax.ShapeDtypeStruct((b, RWKV_PAIRS, RWKV_HEAD_DIM, LANES), F32)],
        scratch_shapes=[pltpu.VMEM((rows, LANES), F32), pltpu.VMEM((rows, LANES), F32)],
        compiler_params=_params(("parallel", "arbitrary")),
        name="rwkv_scan",
    )(r, d, k, v, a, bv, _pair_state(wkv_prev), _head_ones())
    return y, _unpair_state(sf)


def _rwkv_post_kernel(y_ref, bonus_ref, gate_ref, g_ref, b_ref, ones_ref, o_ref):
    ones2 = ones_ref[...]
    inv = 1.0 / RWKV_HEAD_DIM
    for p in range(RWKV_PAIRS):
        sl = slice(p * LANES, (p + 1) * LANES)
        y = y_ref[0, p]
        m = _seg_sum(y, ones2) * inv
        c = y - m
        var = _seg_sum(c * c, ones2) * inv
        out = c * lax.rsqrt(var + RWKV_LN_EPS) * g_ref[:, sl] + b_ref[:, sl] + bonus_ref[0, :, sl]
        o_ref[0, :, sl] = (out * _silu(gate_ref[0, :, sl])).astype(o_ref.dtype)


def _rwkv_post(y, bonus, p3d, lnx_g, lnx_b):
    b, t_len, _ = bonus.shape
    tt = min(t_len, 512)
    vec = pl.BlockSpec((1, BR_W), lambda i, t: (0, 0))
    return pl.pallas_call(
        _rwkv_post_kernel,
        grid=(b, t_len // tt),
        in_specs=[pl.BlockSpec((1, RWKV_PAIRS, tt, LANES), lambda i, t: (i, 0, t, 0)),
                  pl.BlockSpec((1, tt, BR_W), lambda i, t: (i, t, 0)),
                  _row_spec(tt, 'rwkv_gate'),
                  vec, vec,
                  pl.BlockSpec((2 * LANES, LANES), lambda i, t: (0, 0))],
        out_specs=pl.BlockSpec((1, tt, BR_W), lambda i, t: (i, t, 0)),
        out_shape=jax.ShapeDtypeStruct((b, t_len, BR_W), _branch_dtype(tt)),
        compiler_params=_params(("parallel", "parallel")),
        name="rwkv_post",
    )(y, bonus, p3d, lnx_g.reshape(1, BR_W), lnx_b.reshape(1, BR_W), _head_ones())


def _rope_tables(pos):
    def tab(half):
        freqs = ROPE_THETA ** (-jnp.arange(half, dtype=F32) / half)
        ang = pos.astype(F32)[:, None] * freqs[None, :]
        c, s = jnp.cos(ang), jnp.sin(ang)
        reps = LANES // (2 * half)
        return (jnp.tile(jnp.concatenate([c, c], axis=1), (1, reps)),
                jnp.tile(jnp.concatenate([-s, s], axis=1), (1, reps)))
    c128, s128 = tab(ATTN_HEAD_DIM // 2)
    c64, s64 = tab(IDX_DIM // 2)
    return c128, s128, c64, s64


def _rot128(x, cos, sin):
    return x * cos + pltpu.roll(x, ATTN_HEAD_DIM // 2, 1) * sin


def _rot64(x, cos, sin):
    half = IDX_DIM // 2
    lane = lax.broadcasted_iota(I32, x.shape, 1) % IDX_DIM
    partner = jnp.where(lane < half, pltpu.roll(x, LANES - half, 1), pltpu.roll(x, half, 1))
    return x * cos + partner * sin


def _attn_prep_kernel(q_ref, k_ref, v_ref, iq_ref, sm_ref, qn_ref, kn_ref, c128_ref, s128_ref, c64_ref, s64_ref,
                      qo_ref, kf_ref, kb_ref, vf_ref, vb_ref, iqo_ref, kif_ref, kia_ref, kib_ref, wo_ref):
    c128, s128, c64, s64 = c128_ref[...], s128_ref[...], c64_ref[...], s64_ref[...]
    scale = ATTN_HEAD_DIM ** -0.5
    lowp = qo_ref.dtype
    for h in range(ATTN_HEADS):
        sl = slice(h * LANES, (h + 1) * LANES)
        q = q_ref[0, :, sl]
        q = q * lax.rsqrt(jnp.mean(q * q, axis=-1, keepdims=True) + NORM_EPS) * qn_ref[...]
        qo_ref[0, :, sl] = (_rot128(q, c128, s128) * scale).astype(lowp)
        k = k_ref[0, :, sl]
        k = k * lax.rsqrt(jnp.mean(k * k, axis=-1, keepdims=True) + NORM_EPS) * kn_ref[...]
        k = _rot128(k, c128, s128)
        kf_ref[0, :, sl] = k
        kb_ref[0, :, sl] = k.astype(lowp)
        iqo_ref[0, :, sl] = _rot64(iq_ref[0, :, sl], c64, s64).astype(lowp)
    v = v_ref[0]
    vf_ref[0] = v
    vb_ref[0] = v.astype(lowp)
    sm = sm_ref[0, :, LANES:2 * LANES]
    ki = _rot64(sm, c64, s64)
    lane = lax.broadcasted_iota(I32, ki.shape, 1)
    kif_ref[0] = ki[:, 0:IDX_DIM]
    kz = jnp.where(lane < IDX_DIM, ki, 0.0)
    kia_ref[0] = kz.astype(lowp)
    kib_ref[0] = pltpu.roll(kz, IDX_DIM, 1).astype(lowp)
    wo_ref[0] = sm[:, IDX_DIM:IDX_DIM + IDX_HEADS] * IDX_SCALE


def _attn_prep(p3d, pos, qn, kn):
    b, t_len, _ = p3d.shape
    tt = min(t_len, 256)
    lowp = _branch_dtype(tt)
    c128, s128, c64, s64 = _rope_tables(pos)
    tab = pl.BlockSpec((tt, LANES), lambda i, t: (t, 0))
    hv = pl.BlockSpec((1, LANES), lambda i, t: (0, 0))
    o_row = pl.BlockSpec((1, tt, BR_W), lambda i, t: (i, t, 0))
    o_l = pl.BlockSpec((1, tt, LANES), lambda i, t: (i, t, 0))
    big_f = jax.ShapeDtypeStruct((b, t_len, BR_W), F32)
    big_b = jax.ShapeDtypeStruct((b, t_len, BR_W), lowp)
    outs = pl.pallas_call(
        _attn_prep_kernel,
        grid=(b, t_len // tt),
        in_specs=[_row_spec(tt, 'attn_q'), _row_spec(tt, 'attn_k'), _row_spec(tt, 'attn_v'), _row_spec(tt, 'idx_q'),
                  pl.BlockSpec((1, tt, SMALL_W), lambda i, t: (i, t, SMALL_BLK)),
                  hv, hv, tab, tab, tab, tab],
        out_specs=[o_row, o_row, o_row, o_row, o_row, o_row,
                   pl.BlockSpec((1, tt, IDX_DIM), lambda i, t: (i, t, 0)), o_l, o_l,
                   pl.BlockSpec((1, tt, IDX_HEADS), lambda i, t: (i, t, 0))],
        out_shape=[big_b, big_f, big_b, big_f, big_b, big_b,
                   jax.ShapeDtypeStruct((b, t_len, IDX_DIM), F32),
                   jax.ShapeDtypeStruct((b, t_len, LANES), lowp),
                   jax.ShapeDtypeStruct((b, t_len, LANES), lowp),
                   jax.ShapeDtypeStruct((b, t_len, IDX_HEADS), F32)],
        compiler_params=_params(("parallel", "parallel")),
        name="attn_prep",
    )(p3d, p3d, p3d, p3d, p3d, qn.reshape(1, LANES), kn.reshape(1, LANES), c128, s128, c64, s64)
    return tuple(o if o.dtype == F32 else o.astype(BF16) for o in outs)


def _sort_key(x):
    bits = pltpu.bitcast(x, I32)
    return jnp.where(bits < 0, bits ^ jnp.int32(0x7FFFFFFF), bits)


def _kth_largest_key(load_tile, n_tiles, rows, k_top):
    def bit_step(i, lo):
        cand = lo + lax.shift_left(jnp.int32(1), jnp.int32(31) - i)

        def tile_step(j, acc):
            return acc + jnp.where(load_tile(j) >= cand, 1.0, 0.0)

        acc = lax.fori_loop(0, n_tiles, tile_step, jnp.zeros((rows, LANES), F32))
        cnt = jnp.broadcast_to(jnp.sum(acc, axis=-1, keepdims=True), (rows, LANES))
        return jnp.where(cnt >= k_top, cand, lo)

    lo = lax.fori_loop(0, 32, bit_step, jnp.full((rows, LANES), INT_MIN, I32))
    return jnp.maximum(lo, jnp.int32(INT_MIN + 1))


def _attn_prompt_kernel(q_ref, iq_ref, w_ref, gate_ref, k_ref, v_ref, kia_ref, kib_ref, o_ref,
                        key_scr, *, k_top, kc):
    qb = pl.program_id(1)
    n_chunks = ((qb + 1) * Q_BLOCK + kc - 1) // kc
    qpos = qb * Q_BLOCK + lax.broadcasted_iota(I32, (Q_BLOCK, kc), 0)
    w = w_ref[0]

    def score_chunk(c, carry):
        k0 = pl.multiple_of(c * kc, kc)
        ka = kia_ref[0, pl.ds(k0, kc), :]
        kb = kib_ref[0, pl.ds(k0, kc), :]
        sc = jnp.zeros((Q_BLOCK, kc), F32)
        for j in range(IDX_HEADS // 2):
            qp = iq_ref[0, :, j * LANES:(j + 1) * LANES]
            for hh, kk in ((0, ka), (1, kb)):
                rel = lax.dot_general(qp, kk, (((1,), (1,)), ((), ())), preferred_element_type=F32)
                h = 2 * j + hh
                sc = sc + jnp.maximum(rel, 0.0) * w[:, h:h + 1]
        spos = k0 + lax.broadcasted_iota(I32, (Q_BLOCK, kc), 1)
        key_scr[:, pl.ds(k0, kc)] = jnp.where(spos <= qpos, _sort_key(sc), jnp.int32(INT_MIN))
        return carry

    lax.fori_loop(0, n_chunks, score_chunk, 0)

    def load_tile(j):
        return key_scr[:, pl.ds(pl.multiple_of(j * LANES, LANES), LANES)]

    thr = _kth_largest_key(load_tile, n_chunks * (kc // LANES), Q_BLOCK, k_top)
    thr_c = jnp.concatenate([thr] * (kc // LANES), axis=1)

    for h in range(ATTN_HEADS):
        sl = slice(h * LANES, (h + 1) * LANES)
        qh = q_ref[0, :, sl]

        def attend_chunk(c, carry):
            m, l, acc = carry
            k0 = pl.multiple_of(c * kc, kc)
            kh = k_ref[0, pl.ds(k0, kc), sl]
            vh = v_ref[0, pl.ds(k0, kc), sl]
            s = lax.dot_general(qh, kh, (((1,), (1,)), ((), ())), preferred_element_type=F32)
            s = jnp.where(key_scr[:, pl.ds(k0, kc)] >= thr_c, s, NEG_BIG)
            m_new = jnp.maximum(m, jnp.max(s, axis=-1, keepdims=True))
            alpha = jnp.exp(m - m_new)
            p = jnp.exp(s - m_new)
            l = l * alpha + jnp.sum(p, axis=-1, keepdims=True)
            acc = acc * alpha + jnp.dot(p.astype(BF16), vh, preferred_element_type=F32)
            return m_new, l, acc

        m0 = jnp.full((Q_BLOCK, 1), NEG_BIG, F32)
        l0 = jnp.zeros((Q_BLOCK, 1), F32)
        a0 = jnp.zeros((Q_BLOCK, LANES), F32)
        m, l, acc = lax.fori_loop(0, n_chunks, attend_chunk, (m0, l0, a0))
        o_ref[0, :, sl] = (acc / l * _silu(gate_ref[0, :, sl])).astype(BF16)


def _attn_prompt(prep, p3d):
    qb16, _, kb16, _, vb16, iqb16, _, kia, kib, w = prep
    b, t_len, _ = qb16.shape
    k_top = min(TOPK_MAX, t_len // 4)
    kc = min(512, t_len)
    assert t_len % kc == 0 and kc % Q_BLOCK == 0
    qrow = pl.BlockSpec((1, Q_BLOCK, BR_W), lambda i, t: (i, t, 0))

    def full(width):
        return pl.BlockSpec((1, t_len, width), lambda i, t: (i, 0, 0))

    return pl.pallas_call(
        functools.partial(_attn_prompt_kernel, k_top=k_top, kc=kc),
        grid=(b, t_len // Q_BLOCK),
        in_specs=[qrow, qrow,
                  pl.BlockSpec((1, Q_BLOCK, IDX_HEADS), lambda i, t: (i, t, 0)),
                  _row_spec(Q_BLOCK, 'attn_gate'),
                  full(BR_W), full(BR_W), full(LANES), full(LANES)],
        out_specs=pl.BlockSpec((1, Q_BLOCK, BR_W), lambda i, t: (i, t, 0)),
        out_shape=jax.ShapeDtypeStruct((b, t_len, BR_W), BF16),
        scratch_shapes=[pltpu.VMEM((Q_BLOCK, t_len), I32)],
        compiler_params=_params(("parallel", "arbitrary")),
        name="attn_prompt",
    )(qb16, iqb16, w, p3d, kb16, vb16, kia, kib)


def _sample_scores_kernel(pt_ref, qs_ref, w_ref, page_ref, new_ref, o_ref, *, n_pages, t_new):
    p = pl.program_id(1)
    is_new = p == n_pages
    ki = jnp.where(is_new, new_ref[0], page_ref[0, 0].astype(BF16))
    rel = lax.dot_general(qs_ref[0], ki, (((1,), (1,)), ((), ())), preferred_element_type=F32)
    wr = jnp.maximum(rel, 0.0) * w_ref[0]
    sc = jnp.full((SUBLANES, PAGE_SIZE), -jnp.inf, F32)
    for t in range(t_new):
        sc = _put_row(sc, jnp.sum(wr[t * IDX_HEADS:(t + 1) * IDX_HEADS], axis=0, keepdims=True), t)
    qi = lax.broadcasted_iota(I32, sc.shape, 0)
    si = lax.broadcasted_iota(I32, sc.shape, 1)
    visible = jnp.logical_and(qi < t_new, jnp.logical_or(jnp.logical_not(is_new), si <= qi))
    o_ref[0, 0] = jnp.where(visible, _sort_key(sc), jnp.int32(INT_MIN))


def _sample_thresh_kernel(key_ref, o_ref, *, n_tiles, k_top):
    def load_tile(j):
        return key_ref[0, j]

    o_ref[0] = _kth_largest_key(load_tile, n_tiles, SUBLANES, k_top)


def _sample_attend_kernel(pt_ref, q_ref, key_ref, thr_ref, gate_ref, kp_ref, vp_ref, kn_ref, vn_ref, o_ref,
                          m_scr, l_scr, acc_scr, *, n_pages, t_new):
    p = pl.program_id(1)

    @pl.when(p == 0)
    def _():
        m_scr[...] = jnp.full(m_scr.shape, NEG_BIG, F32)
        l_scr[...] = jnp.zeros(l_scr.shape, F32)
        acc_scr[...] = jnp.zeros(acc_scr.shape, F32)

    is_new = p == n_pages
    kp = jnp.where(is_new, kn_ref[0], kp_ref[0, 0].astype(BF16))
    vp = jnp.where(is_new, vn_ref[0], vp_ref[0, 0].astype(BF16))
    s = lax.dot_general(q_ref[0], kp, (((1,), (1,)), ((), ())), preferred_element_type=F32)
    keys = key_ref[0, 0]
    thr = thr_ref[0]
    spread = lambda x: jnp.concatenate(
        [jnp.broadcast_to(x[t:t + 1, :], (ATTN_HEADS, PAGE_SIZE)) for t in range(t_new)], axis=0)
    s = jnp.where(spread(keys) >= spread(thr), s, NEG_BIG)
    m = m_scr[...]
    m_new = jnp.maximum(m, jnp.max(s, axis=-1, keepdims=True))
    alpha = jnp.exp(m - m_new)
    pr = jnp.exp(s - m_new)
    l_scr[...] = l_scr[...] * alpha + jnp.sum(pr, axis=-1, keepdims=True)
    acc_scr[...] = acc_scr[...] * alpha + jnp.dot(pr.astype(BF16), vp, preferred_element_type=F32)
    m_scr[...] = m_new

    @pl.when(is_new)
    def _():
        out = acc_scr[...] / l_scr[...]
        ri = lax.broadcasted_iota(I32, out.shape, 0) % ATTN_HEADS
        ci = lax.broadcasted_iota(I32, out.shape, 1) // ATTN_HEAD_DIM
        own = jnp.where(ri == ci, out, 0.0)
        res = jnp.zeros((t_new, BR_W), F32)
        for t in range(t_new):
            res = _put_row(res, jnp.sum(own[t * ATTN_HEADS:(t + 1) * ATTN_HEADS], axis=0, keepdims=True), t)
        o_ref[0] = res * _silu(gate_ref[0])


def _attn_sample(prep, p3d, cache_k, cache_v, cache_kidx, page_table, layer):
    qb16, _, kb16, _, vb16, iqb16, kif, _, _, w = prep
    b, t_new, _ = qb16.shape
    assert t_new <= SUBLANES
    n_pages = page_table.shape[1]
    past = n_pages * PAGE_SIZE
    k_top = min(TOPK_MAX, (past + t_new) // 4)
    n_pool = cache_k.shape[0]
    ck = cache_k.reshape(n_pool, DEPTH, PAGE_SIZE, BR_W)
    cv = cache_v.reshape(n_pool, DEPTH, PAGE_SIZE, BR_W)
    pad_rows = lambda x: jnp.pad(x, ((0, 0), (0, PAGE_SIZE - t_new), (0, 0)))
    n_qrow = t_new * IDX_HEADS
    qs = iqb16.reshape(b, n_qrow, IDX_DIM)
    w128 = jnp.broadcast_to(w.reshape(b, n_qrow, 1), (b, n_qrow, PAGE_SIZE))
    ki_new = pad_rows(kif).astype(BF16)
    page = lambda i, p, pt: (pt[i, jnp.minimum(p, n_pages - 1)], layer, 0, 0)
    key_spec = pl.BlockSpec((1, 1, SUBLANES, PAGE_SIZE), lambda i, p, pt: (i, p, 0, 0))

    keys = pl.pallas_call(
        functools.partial(_sample_scores_kernel, n_pages=n_pages, t_new=t_new),
        grid_spec=pltpu.PrefetchScalarGridSpec(
            num_scalar_prefetch=1,
            grid=(b, n_pages + 1),
            in_specs=[pl.BlockSpec((1, n_qrow, IDX_DIM), lambda i, p, pt: (i, 0, 0)),
                      pl.BlockSpec((1, n_qrow, PAGE_SIZE), lambda i, p, pt: (i, 0, 0)),
                      pl.BlockSpec((1, 1, PAGE_SIZE, IDX_DIM), page),
                      pl.BlockSpec((1, PAGE_SIZE, IDX_DIM), lambda i, p, pt: (i, 0, 0))],
            out_specs=key_spec),
        out_shape=jax.ShapeDtypeStruct((b, n_pages + 1, SUBLANES, PAGE_SIZE), I32),
        compiler_params=_params(("parallel", "arbitrary")),
        name="sample_scores",
    )(page_table, qs, w128, cache_kidx, ki_new)

    thr = pl.pallas_call(
        functools.partial(_sample_thresh_kernel, n_tiles=n_pages + 1, k_top=k_top),
        grid=(b,),
        in_specs=[pl.BlockSpec((1, n_pages + 1, SUBLANES, PAGE_SIZE), lambda i: (i, 0, 0, 0))],
        out_specs=pl.BlockSpec((1, SUBLANES, LANES), lambda i: (i, 0, 0)),
        out_shape=jax.ShapeDtypeStruct((b, SUBLANES, LANES), I32),
        compiler_params=_params(("parallel",)),
        name="sample_thresh",
    )(keys)

    q4 = qb16.reshape(b, t_new, ATTN_HEADS, ATTN_HEAD_DIM)
    eye = jnp.eye(ATTN_HEADS, dtype=BF16)
    rows = t_new * ATTN_HEADS
    qbd = (q4[:, :, :, None, :] * eye[None, None, :, :, None]).reshape(b, rows, BR_W)
    out = pl.pallas_call(
        functools.partial(_sample_attend_kernel, n_pages=n_pages, t_new=t_new),
        grid_spec=pltpu.PrefetchScalarGridSpec(
            num_scalar_prefetch=1,
            grid=(b, n_pages + 1),
            in_specs=[pl.BlockSpec((1, rows, BR_W), lambda i, p, pt: (i, 0, 0)),
                      key_spec,
                      pl.BlockSpec((1, SUBLANES, LANES), lambda i, p, pt: (i, 0, 0)),
                      pl.BlockSpec((1, t_new, BR_W), lambda i, p, pt: (i, 0, _cblk('attn_gate'))),
                      pl.BlockSpec((1, 1, PAGE_SIZE, BR_W), page),
                      pl.BlockSpec((1, 1, PAGE_SIZE, BR_W), page),
                      pl.BlockSpec((1, PAGE_SIZE, BR_W), lambda i, p, pt: (i, 0, 0)),
                      pl.BlockSpec((1, PAGE_SIZE, BR_W), lambda i, p, pt: (i, 0, 0))],
            out_specs=pl.BlockSpec((1, t_new, BR_W), lambda i, p, pt: (i, 0, 0)),
            scratch_shapes=[pltpu.VMEM((rows, 1), F32), pltpu.VMEM((rows, 1), F32),
                            pltpu.VMEM((rows, BR_W), F32)]),
        out_shape=jax.ShapeDtypeStruct((b, t_new, BR_W), F32),
        compiler_params=_params(("parallel", "arbitrary")),
        name="sample_attend",
    )(page_table, qbd, keys, thr, p3d, ck, cv, pad_rows(kb16), pad_rows(vb16))
    return out


def _layer(x, lw, pos0, pool_prev, shift_prev, wkv_prev, paged):
    b, t_len, _ = x.shape
    x2d = x.reshape(b * t_len, D_MODEL)
    pos = pos0 + jnp.arange(t_len, dtype=I32)
    p2d = _inproj(x2d, lw['norm_g'], lw['w_in'])
    p3d = p2d.reshape(b, t_len, IN_COLS_PAD)

    pool_br, pool_state = _pool(p3d, pool_prev, lw['pool_w'], lw['pool_scale'], pos0)
    gmlp_br, gmlp_vn = _gmlp(p3d, lw['gmlp_ln_g'], lw['gmlp_ln_b'], lw['gmlp_wm'], lw['gmlp_bsb'],
                             emit_vn=paged is not None)
    seqs, shift_state = _rwkv_prep(p3d, shift_prev, lw)
    y, wkv_state = _rwkv_scan(seqs[:6], wkv_prev)
    rwkv_br = _rwkv_post(y, seqs[6], p3d, lw['rwkv_lnx_g'], lw['rwkv_lnx_b'])
    prep = _attn_prep(p3d, pos, lw['attn_qn'], lw['attn_kn'])
    if paged is None:
        attn_br = _attn_prompt(prep, p3d)
    else:
        attn_br = _attn_sample(prep, p3d, *paged)

    flat = lambda t: t.reshape(b * t_len, BR_W).astype(BF16)
    merged = _merge([flat(pool_br), flat(gmlp_br), flat(rwkv_br), flat(attn_br)], lw['w_branch'], p2d)
    y_out = _outproj(x2d, merged, lw['w_out']).reshape(b, t_len, D_MODEL)
    hd = lambda t: t.reshape(b, t_len, ATTN_HEADS, ATTN_HEAD_DIM)
    return y_out, (hd(prep[1]), hd(prep[3]), prep[6], pool_state, shift_state, wkv_state, gmlp_vn)


def _lora_weights(w2, a2):
    z = jnp.zeros_like(w2)
    return jnp.concatenate([w2, z], axis=0).astype(BF16), jnp.concatenate([z, a2], axis=0).astype(BF16)


def kernel(x_prompt, x_sample, cache_k, cache_v, cache_kidx, page_table, state_pool, state_shift, state_wkv,
           norm_g, w_in, pool_w, pool_scale, gmlp_ln_g, gmlp_ln_b, gmlp_ws, gmlp_bs, rwkv_mu, rwkv_w0, rwkv_w2,
           rwkv_a0, rwkv_a2, rwkv_kk, rwkv_ka, rwkv_rk, rwkv_lnx_g, rwkv_lnx_b, attn_qn, attn_kn, w_branch, w_out):
    past = page_table.shape[1] * PAGE_SIZE
    bp = x_prompt.shape[0]
    w_in_p = _permute_w_in(w_in)
    w_branch_b = w_branch.astype(BF16)
    w_out_b = w_out.astype(BF16)
    pool_w_b = pool_w.astype(BF16)
    causal = jnp.tril(jnp.ones((GMLP_CHUNK, GMLP_CHUNK), dtype=bool))
    gmlp_wm = jnp.where(causal[None, None], gmlp_ws, 0.0).astype(BF16)
    gmlp_bsb = jnp.broadcast_to(gmlp_bs[..., None], gmlp_bs.shape + (GMLP_GROUP_W,))
    xp, xs = x_prompt, x_sample
    outs_p, outs_s = [], []
    for l in range(DEPTH):
        wl, al = _lora_weights(rwkv_w2[l], rwkv_a2[l])
        lw = {
            'norm_g': norm_g[l], 'w_in': w_in_p[l], 'pool_w': pool_w_b[l], 'pool_scale': pool_scale[l],
            'gmlp_ln_g': gmlp_ln_g[l], 'gmlp_ln_b': gmlp_ln_b[l], 'gmlp_wm': gmlp_wm[l], 'gmlp_bsb': gmlp_bsb[l],
            'rwkv_mu': rwkv_mu[l], 'rwkv_w0': rwkv_w0[l], 'rwkv_wl': wl, 'rwkv_a0': rwkv_a0[l],
            'rwkv_al': al, 'rwkv_kk': rwkv_kk[l], 'rwkv_ka': rwkv_ka[l], 'rwkv_rk': rwkv_rk[l].reshape(BR_W),
            'rwkv_lnx_g': rwkv_lnx_g[l], 'rwkv_lnx_b': rwkv_lnx_b[l], 'attn_qn': attn_qn[l],
            'attn_kn': attn_kn[l], 'w_branch': w_branch_b[l], 'w_out': w_out_b[l],
        }
        xp, st_p = _layer(
            xp, lw, 0,
            jnp.zeros((bp, POOL_STATE, BR_W), F32),
            jnp.zeros((bp, RWKV_SHIFT_W), F32),
            jnp.zeros((bp, RWKV_HEADS, RWKV_HEAD_DIM, RWKV_HEAD_DIM), F32),
            None)
        xs, st_s = _layer(xs, lw, past, state_pool[:, l], state_shift[:, l], state_wkv[:, l],
                          (cache_k, cache_v, cache_kidx, page_table, l))
        outs_p.append(st_p)
        outs_s.append(st_s)
    stk = lambda outs, i: jnp.stack([o[i] for o in outs], axis=1)
    k_p, v_p, ki_p = stk(outs_p, 0), stk(outs_p, 1), stk(outs_p, 2)
    pool_p, shift_p, wkv_p = stk(outs_p, 3), stk(outs_p, 4), stk(outs_p, 5)
    k_s, v_s, ki_s = stk(outs_s, 0), stk(outs_s, 1), stk(outs_s, 2)
    pool_s, shift_s, wkv_s = stk(outs_s, 3), stk(outs_s, 4), stk(outs_s, 5)
    gmlp_v_s = stk(outs_s, 6)
    return (xp, xs, k_p, v_p, ki_p, k_s, v_s, ki_s, pool_p, pool_s, shift_p, shift_s, wkv_p, wkv_s, gmlp_v_s)
```

```python
import functools

import numpy as np
import jax
import jax.numpy as jnp
from jax import lax
from jax.experimental import pallas as pl
from jax.experimental.pallas import tpu as pltpu

F32 = jnp.float32
BF16 = jnp.bfloat16
I32 = jnp.int32

D_MODEL = 2048
DEPTH = 4
PAGE_SIZE = 128
BR_W = D_MODEL // 2
N_BRANCH = 4
POOL_WINDOWS = (2, 4, 8, 16)
POOL_GROUPS = 4
POOL_GROUP_W = BR_W // POOL_GROUPS
POOL_STATE = 15
POOL_HALO = 16
GMLP_CHUNK = 128
GMLP_GROUPS = 8
GMLP_GROUP_W = BR_W // GMLP_GROUPS
GMLP_LN_EPS = 1e-5
RWKV_HEAD_DIM = 64
RWKV_HEADS = BR_W // RWKV_HEAD_DIM
RWKV_PAIRS = RWKV_HEADS // 2
RWKV_W_LORA = 64
RWKV_A_LORA = 64
RWKV_SHIFT_W = 3 * BR_W + RWKV_W_LORA + RWKV_A_LORA
RWKV_LN_EPS = 64e-5
ATTN_HEAD_DIM = 128
ATTN_HEADS = BR_W // ATTN_HEAD_DIM
IDX_HEADS = 16
IDX_DIM = 64
IDX_SCALE = (IDX_HEADS * IDX_DIM) ** -0.5
TOPK_MAX = 256
Q_BLOCK = 128
ROPE_THETA = 10000.0
NORM_EPS = 1e-6
LANES = 128
SUBLANES = 8
BF16_ROWS = 16
INT_MIN = -2 ** 31
NEG_BIG = -1e30

SEG_SRC = (
    ('pool_u', BR_W), ('pool_gate', BR_W),
    ('gmlp_u', BR_W), ('gmlp_v', BR_W), ('gmlp_gate', BR_W),
    ('rwkv_shift', RWKV_SHIFT_W), ('rwkv_gate', BR_W),
    ('attn_q', BR_W), ('attn_k', BR_W), ('attn_v', BR_W),
    ('idx_q', IDX_HEADS * IDX_DIM), ('idx_k', IDX_DIM), ('idx_w', IDX_HEADS),
    ('attn_gate', BR_W), ('merge', N_BRANCH * D_MODEL),
)
SEG_DST_ORDER = ('pool_u', 'pool_gate', 'gmlp_u', 'gmlp_v', 'gmlp_gate', 'rwkv_gate',
                 'attn_q', 'attn_k', 'attn_v', 'idx_q', 'attn_gate', 'merge',
                 'rwkv_shift', 'idx_k', 'idx_w')
IN_COLS = sum(w for _, w in SEG_SRC)
IN_COLS_PAD = 23040
VMEM_LIMIT = 56 * 1024 * 1024


def _seg_offsets():
    src, o = {}, 0
    for name, w in SEG_SRC:
        src[name] = (o, w)
        o += w
    dst, o = {}, 0
    for name in SEG_DST_ORDER:
        dst[name] = o
        o += src[name][1]
    return src, dst


SRC_OFF, DST_OFF = _seg_offsets()
SMALL_W = 256
SMALL_BLK = (DST_OFF['rwkv_shift'] + 3 * BR_W) // SMALL_W
assert (DST_OFF['rwkv_shift'] + 3 * BR_W) % SMALL_W == 0
assert DST_OFF['idx_k'] == SMALL_BLK * SMALL_W + 128 and DST_OFF['idx_w'] == SMALL_BLK * SMALL_W + 192
assert RWKV_W_LORA + RWKV_A_LORA == LANES


def _cblk(name):
    assert DST_OFF[name] % BR_W == 0
    return DST_OFF[name] // BR_W


def _permute_w_in(w_in):
    parts = [w_in[..., SRC_OFF[n][0]:SRC_OFF[n][0] + SRC_OFF[n][1]] for n in SEG_DST_ORDER]
    parts.append(jnp.zeros(w_in.shape[:-1] + (IN_COLS_PAD - IN_COLS,), w_in.dtype))
    return jnp.concatenate(parts, axis=-1).astype(BF16)


def _seg(p, name):
    return p[..., DST_OFF[name]:DST_OFF[name] + SRC_OFF[name][1]]


def _params(sem):
    return pltpu.CompilerParams(dimension_semantics=sem, vmem_limit_bytes=VMEM_LIMIT)


def _branch_dtype(tt):
    return BF16 if tt % BF16_ROWS == 0 else F32


def _row_spec(tt, name):
    c = _cblk(name)
    return pl.BlockSpec((1, tt, BR_W), lambda i, t: (i, t, c))


def _silu(x):
    return x * jax.nn.sigmoid(x)


def _head_ones():
    r = np.arange(2 * LANES)[:, None] % LANES
    c = np.arange(LANES)[None, :]
    return jnp.asarray((r // RWKV_HEAD_DIM) == (c // RWKV_HEAD_DIM), dtype=BF16)


def _seg_sum(x, ones2):
    hi = x.astype(BF16)
    lo = (x - hi.astype(F32)).astype(BF16)
    return jnp.dot(jnp.concatenate([hi, lo], axis=1), ones2, preferred_element_type=F32)


def _put_row(dst, row, i):
    sub = lax.broadcasted_iota(I32, dst.shape, 0)
    return jnp.where(sub == i, jnp.broadcast_to(row, dst.shape), dst)


def _inproj_kernel(x_ref, g_ref, w_ref, o_ref, h_scr):
    @pl.when(pl.program_id(1) == 0)
    def _():
        x = x_ref[...]
        ms = jnp.mean(x * x, axis=-1, keepdims=True)
        h_scr[...] = (x * lax.rsqrt(ms + NORM_EPS) * g_ref[...]).astype(BF16)

    o_ref[...] = jnp.dot(h_scr[...], w_ref[...], preferred_element_type=F32)


def _inproj(x2d, g, w_bf16):
    m = x2d.shape[0]
    tm = min(m, 1024)
    tn = 512
    return pl.pallas_call(
        _inproj_kernel,
        grid=(m // tm, IN_COLS_PAD // tn),
        in_specs=[pl.BlockSpec((tm, D_MODEL), lambda i, j: (i, 0)),
                  pl.BlockSpec((1, D_MODEL), lambda i, j: (0, 0)),
                  pl.BlockSpec((D_MODEL, tn), lambda i, j: (0, j))],
        out_specs=pl.BlockSpec((tm, tn), lambda i, j: (i, j)),
        out_shape=jax.ShapeDtypeStruct((m, IN_COLS_PAD), F32),
        scratch_shapes=[pltpu.VMEM((tm, D_MODEL), BF16)],
        compiler_params=_params(("parallel", "arbitrary")),
        name="inproj",
    )(x2d, g.reshape(1, D_MODEL), w_bf16)


def _merge_kernel(b0, b1, b2, b3, wb_ref, g0, g1, g2, g3, o_ref):
    acc = None
    for n, (b_ref, g_ref) in enumerate(((b0, g0), (b1, g1), (b2, g2), (b3, g3))):
        proj = jnp.dot(b_ref[...], wb_ref[n], preferred_element_type=F32)
        term = jax.nn.sigmoid(g_ref[...]) * proj
        acc = term if acc is None else acc + term
    o_ref[...] = acc.astype(BF16)


def _merge(branches, wb_bf16, p2d):
    m = p2d.shape[0]
    tm = min(m, 512)
    tn = 512
    goff = DST_OFF['merge'] // tn
    gstep = D_MODEL // tn
    bspec = pl.BlockSpec((tm, BR_W), lambda i, j: (i, 0))

    def gate_map(n):
        return lambda i, j: (i, goff + n * gstep + j)

    gspecs = [pl.BlockSpec((tm, tn), gate_map(n)) for n in range(N_BRANCH)]
    return pl.pallas_call(
        _merge_kernel,
        grid=(m // tm, D_MODEL // tn),
        in_specs=[bspec, bspec, bspec, bspec,
                  pl.BlockSpec((N_BRANCH, BR_W, tn), lambda i, j: (0, 0, j))] + gspecs,
        out_specs=pl.BlockSpec((tm, tn), lambda i, j: (i, j)),
        out_shape=jax.ShapeDtypeStruct((m, D_MODEL), BF16),
        compiler_params=_params(("parallel", "arbitrary")),
        name="merge",
    )(*branches, wb_bf16, p2d, p2d, p2d, p2d)


def _outproj_kernel(x_ref, m_ref, w_ref, o_ref):
    o_ref[...] = x_ref[...] + jnp.dot(m_ref[...], w_ref[...], preferred_element_type=F32)


def _outproj(x2d, merged, wo_bf16):
    m = x2d.shape[0]
    tm = min(m, 1024)
    tn = 512
    return pl.pallas_call(
        _outproj_kernel,
        grid=(m // tm, D_MODEL // tn),
        in_specs=[pl.BlockSpec((tm, tn), lambda i, j: (i, j)),
                  pl.BlockSpec((tm, D_MODEL), lambda i, j: (i, 0)),
                  pl.BlockSpec((D_MODEL, tn), lambda i, j: (0, j))],
        out_specs=pl.BlockSpec((tm, tn), lambda i, j: (i, j)),
        out_shape=jax.ShapeDtypeStruct((m, D_MODEL), F32),
        compiler_params=_params(("parallel", "arbitrary")),
        name="outproj",
    )(x2d, merged, wo_bf16)


def _pool_kernel(u_ref, gate_ref, prev_ref, w_ref, scale_ref, o_ref, st_ref, ext, *, tt, pos0):
    t = pl.program_id(1)
    nt = pl.num_programs(1)

    @pl.when(t == 0)
    def _():
        ext[0:1, :] = jnp.zeros((1, BR_W), F32)
        ext[1:POOL_HALO, :] = prev_ref[0]

    @pl.when(t > 0)
    def _():
        ext[0:POOL_HALO, :] = ext[tt:tt + POOL_HALO, :]

    ext[POOL_HALO:POOL_HALO + tt, :] = u_ref[0]
    pos = pos0 + t * tt + lax.broadcasted_iota(I32, (tt, POOL_GROUP_W), 0)
    for g, w in enumerate(POOL_WINDOWS):
        sl = slice(g * POOL_GROUP_W, (g + 1) * POOL_GROUP_W)
        cur = ext[POOL_HALO:POOL_HALO + tt, sl]
        acc = cur
        for i in range(1, w):
            acc = acc + ext[POOL_HALO - i:POOL_HALO - i + tt, sl]
        cnt = jnp.minimum(w, pos + 1).astype(F32)
        pooled = acc / cnt - cur
        mixed = jnp.dot(pooled.astype(BF16), w_ref[g], preferred_element_type=F32)
        o_ref[0, :, sl] = (mixed * scale_ref[:, sl] * _silu(gate_ref[0, :, sl])).astype(o_ref.dtype)

    @pl.when(t == nt - 1)
    def _():
        st_ref[0] = ext[tt + 1:tt + POOL_HALO, :]


def _pool(p3d, prev, pool_w_bf16, scale, pos0):
    b, t_len, _ = p3d.shape
    tt = min(t_len, 512)
    return pl.pallas_call(
        functools.partial(_pool_kernel, tt=tt, pos0=pos0),
        grid=(b, t_len // tt),
        in_specs=[_row_spec(tt, 'pool_u'), _row_spec(tt, 'pool_gate'),
                  pl.BlockSpec((1, POOL_STATE, BR_W), lambda i, t: (i, 0, 0)),
                  pl.BlockSpec((POOL_GROUPS, POOL_GROUP_W, POOL_GROUP_W), lambda i, t: (0, 0, 0)),
                  pl.BlockSpec((1, BR_W), lambda i, t: (0, 0))],
        out_specs=[pl.BlockSpec((1, tt, BR_W), lambda i, t: (i, t, 0)),
                   pl.BlockSpec((1, POOL_STATE, BR_W), lambda i, t: (i, 0, 0))],
        out_shape=[jax.ShapeDtypeStruct((b, t_len, BR_W), _branch_dtype(tt)),
                   jax.ShapeDtypeStruct((b, POOL_STATE, BR_W), F32)],
        scratch_shapes=[pltpu.VMEM((POOL_HALO + tt, BR_W), F32)],
        compiler_params=_params(("parallel", "arbitrary")),
        name="pool",
    )(p3d, p3d, prev, pool_w_bf16, scale.reshape(1, BR_W))


def _gmlp_kernel(u_ref, v_ref, gate_ref, lng_ref, lnb_ref, wm_ref, bsb_ref, o_ref, *rest, tt, emit_vn):
    if emit_vn:
        vn_ref = rest[0]
        rest = rest[1:]
    v = v_ref[0]
    mu = jnp.mean(v, axis=-1, keepdims=True)
    var = jnp.mean(jnp.square(v - mu), axis=-1, keepdims=True)
    vn = (v - mu) * lax.rsqrt(var + GMLP_LN_EPS) * lng_ref[...] + lnb_ref[...]
    if emit_vn:
        vn_ref[0] = vn
    if tt % GMLP_CHUNK == 0:
        rows = GMLP_CHUNK
        n_chunks = tt // GMLP_CHUNK
        vnb = vn.astype(BF16)
    else:
        pad = rest[0]
        rows = tt
        n_chunks = 1
        pad[...] = jnp.zeros_like(pad)
        pad[0:tt, :] = vn
        vnb = pad[...].astype(BF16)
    for c in range(n_chunks):
        for g in range(GMLP_GROUPS):
            sl = slice(g * GMLP_GROUP_W, (g + 1) * GMLP_GROUP_W)
            rs = slice(c * rows, (c + 1) * rows)
            vc = vnb[c * GMLP_CHUNK:(c + 1) * GMLP_CHUNK, sl]
            mixed = jnp.dot(wm_ref[g], vc, preferred_element_type=F32) + bsb_ref[g]
            o_ref[0, rs, sl] = (u_ref[0, rs, sl] * mixed[0:rows] * _silu(gate_ref[0, rs, sl])).astype(o_ref.dtype)


def _gmlp(p3d, ln_g, ln_b, wm_bf16, bsb, emit_vn):
    b, t_len, _ = p3d.shape
    tt = min(t_len, 512)
    assert tt % GMLP_CHUNK == 0 or (tt == t_len and tt < GMLP_CHUNK)
    out_specs = [pl.BlockSpec((1, tt, BR_W), lambda i, t: (i, t, 0))]
    out_shape = [jax.ShapeDtypeStruct((b, t_len, BR_W), _branch_dtype(tt))]
    if emit_vn:
        out_specs.append(pl.BlockSpec((1, tt, BR_W), lambda i, t: (i, t, 0)))
        out_shape.append(jax.ShapeDtypeStruct((b, t_len, BR_W), F32))
    scratch = [] if tt % GMLP_CHUNK == 0 else [pltpu.VMEM((GMLP_CHUNK, BR_W), F32)]
    res = pl.pallas_call(
        functools.partial(_gmlp_kernel, tt=tt, emit_vn=emit_vn),
        grid=(b, t_len // tt),
        in_specs=[_row_spec(tt, 'gmlp_u'), _row_spec(tt, 'gmlp_v'), _row_spec(tt, 'gmlp_gate'),
                  pl.BlockSpec((1, BR_W), lambda i, t: (0, 0)),
                  pl.BlockSpec((1, BR_W), lambda i, t: (0, 0)),
                  pl.BlockSpec((GMLP_GROUPS, GMLP_CHUNK, GMLP_CHUNK), lambda i, t: (0, 0, 0)),
                  pl.BlockSpec((GMLP_GROUPS, GMLP_CHUNK, GMLP_GROUP_W), lambda i, t: (0, 0, 0))],
        out_specs=out_specs,
        out_shape=out_shape,
        scratch_shapes=scratch,
        compiler_params=_params(("parallel", "parallel")),
        name="gmlp",
    )(p3d, p3d, p3d, ln_g.reshape(1, BR_W), ln_b.reshape(1, BR_W), wm_bf16, bsb)
    return (res[0], res[1]) if emit_vn else (res[0], None)


def _rwkv_prep_kernel(r_ref, k_ref, v_ref, sm_ref, prev_ref, mu_ref, w0_ref, a0_ref, wl_ref, al_ref,
                      kk_ref, ka_ref, rk_ref, ones_ref,
                      ro_ref, do_ref, ko_ref, vo_ref, ao_ref, bo_ref, bonus_ref, st_ref,
                      buf, *, tt):
    t = pl.program_id(1)
    nt = pl.num_programs(1)

    @pl.when(t == 0)
    def _():
        buf[7:8, :] = prev_ref[0]

    @pl.when(t > 0)
    def _():
        buf[7:8, :] = buf[7 + tt:8 + tt, :]

    buf[8:8 + tt, 0:BR_W] = r_ref[0]
    buf[8:8 + tt, BR_W:2 * BR_W] = k_ref[0]
    buf[8:8 + tt, 2 * BR_W:3 * BR_W] = v_ref[0]
    buf[8:8 + tt, 3 * BR_W:RWKV_SHIFT_W] = sm_ref[0, :, 0:LANES]

    @pl.when(t == nt - 1)
    def _():
        st_ref[0] = buf[7 + tt:8 + tt, :]

    ones2 = ones_ref[...]

    def mixed(lo, hi):
        cur = buf[8:8 + tt, lo:hi]
        prv = buf[7:7 + tt, lo:hi]
        return cur + (prv - cur) * mu_ref[:, lo:hi]

    zs = mixed(3 * BR_W, RWKV_SHIFT_W)
    w_pre = w0_ref[...] + jnp.dot(jnp.tanh(zs).astype(BF16), wl_ref[...], preferred_element_type=F32)
    w = -jax.nn.softplus(-w_pre) - 0.5
    do_ref[0] = jnp.exp(-jnp.exp(w))
    a = jax.nn.sigmoid(a0_ref[...] + jnp.dot(zs.astype(BF16), al_ref[...], preferred_element_type=F32))
    r = mixed(0, BR_W)
    k = mixed(BR_W, 2 * BR_W)
    v = mixed(2 * BR_W, 3 * BR_W)
    ro_ref[0] = r
    vo_ref[0] = v
    k2 = k * (1.0 + (a - 1.0) * ka_ref[...])
    ko_ref[0] = k2
    kk = k * kk_ref[...]
    rkk = r * k2 * rk_ref[...]
    for j in range(RWKV_PAIRS):
        sl = slice(j * LANES, (j + 1) * LANES)
        kkj = kk[:, sl]
        nrm = jnp.sqrt(_seg_sum(kkj * kkj, ones2))
        kkn = kkj / jnp.maximum(nrm, 1e-12)
        ao_ref[0, :, sl] = -kkn
        bo_ref[0, :, sl] = kkn * a[:, sl]
        bonus_ref[0, :, sl] = _seg_sum(rkk[:, sl], ones2) * v[:, sl]


def _rwkv_prep(p3d, shift_prev, lw):
    b, t_len, _ = p3d.shape
    tt = min(t_len, 256)
    base = _cblk('rwkv_shift')

    def row(c):
        return pl.BlockSpec((1, tt, BR_W), lambda i, t: (i, t, c))

    vec = pl.BlockSpec((1, BR_W), lambda i, t: (0, 0))
    out_row = pl.BlockSpec((1, tt, BR_W), lambda i, t: (i, t, 0))
    big = jax.ShapeDtypeStruct((b, t_len, BR_W), F32)
    outs = pl.pallas_call(
        functools.partial(_rwkv_prep_kernel, tt=tt),
        grid=(b, t_len // tt),
        in_specs=[row(base), row(base + 1), row(base + 2),
                  pl.BlockSpec((1, tt, SMALL_W), lambda i, t: (i, t, SMALL_BLK)),
                  pl.BlockSpec((1, 1, RWKV_SHIFT_W), lambda i, t: (i, 0, 0)),
                  pl.BlockSpec((1, RWKV_SHIFT_W), lambda i, t: (0, 0)),
                  vec, vec,
                  pl.BlockSpec((LANES, BR_W), lambda i, t: (0, 0)),
                  pl.BlockSpec((LANES, BR_W), lambda i, t: (0, 0)),
                  vec, vec, vec,
                  pl.BlockSpec((2 * LANES, LANES), lambda i, t: (0, 0))],
        out_specs=[out_row] * 7 + [pl.BlockSpec((1, 1, RWKV_SHIFT_W), lambda i, t: (i, 0, 0))],
        out_shape=[big] * 7 + [jax.ShapeDtypeStruct((b, 1, RWKV_SHIFT_W), F32)],
        scratch_shapes=[pltpu.VMEM((8 + tt, RWKV_SHIFT_W), F32)],
        compiler_params=_params(("parallel", "arbitrary")),
        name="rwkv_prep",
    )(p3d, p3d, p3d, p3d, shift_prev.reshape(b, 1, RWKV_SHIFT_W), lw['rwkv_mu'].reshape(1, RWKV_SHIFT_W),
      lw['rwkv_w0'].reshape(1, BR_W), lw['rwkv_a0'].reshape(1, BR_W), lw['rwkv_wl'], lw['rwkv_al'],
      lw['rwkv_kk'].reshape(1, BR_W), lw['rwkv_ka'].reshape(1, BR_W), lw['rwkv_rk'].reshape(1, BR_W),
      _head_ones())
    return outs[:7], outs[7].reshape(b, RWKV_SHIFT_W)


def _rwkv_scan_kernel(r_ref, d_ref, k_ref, v_ref, a_ref, b_ref, s0_ref, ones_ref, y_ref, sf_ref,
                      s_scr, dm_scr, dmb_scr, *, nb, tc):
    t = pl.program_id(1)
    nt = pl.num_programs(1)
    rows = nb * RWKV_PAIRS * RWKV_HEAD_DIM
    grp = min(SUBLANES, tc)

    @pl.when(t == 0)
    def _():
        s_scr[...] = s0_ref[...].reshape(rows, LANES)
        ri = lax.broadcasted_iota(I32, (rows, LANES), 0) % RWKV_HEAD_DIM
        ci = lax.broadcasted_iota(I32, (rows, LANES), 1) % RWKV_HEAD_DIM
        dm_scr[...] = jnp.where(ri == ci, 1.0, 0.0)
        dmb_scr[...] = jnp.where(ri == ci, 1.0, 0.0).astype(BF16)

    ones2 = ones_ref[...]

    def spread(tile, i, dtype=F32):
        pieces = []
        for bb in range(nb):
            for p in range(RWKV_PAIRS):
                row = tile[bb, i:i + 1, p * LANES:(p + 1) * LANES].astype(dtype)
                pieces.append(jnp.broadcast_to(row, (RWKV_HEAD_DIM, LANES)))
        return jnp.concatenate(pieces, axis=0)

    def group(gi, carry):
        g0 = pl.multiple_of(gi * grp, grp)
        rt = r_ref[:, pl.ds(g0, grp), :]
        dt = d_ref[:, pl.ds(g0, grp), :]
        kt = k_ref[:, pl.ds(g0, grp), :]
        vt = v_ref[:, pl.ds(g0, grp), :]
        at = a_ref[:, pl.ds(g0, grp), :]
        bt = b_ref[:, pl.ds(g0, grp), :]
        vt_hi = vt.astype(BF16).astype(F32)
        vt_lo = vt - vt_hi
        ys = [jnp.zeros((grp, LANES), F32) for _ in range(nb * RWKV_PAIRS)]
        for i in range(grp):
            dm = dm_scr[...]
            s = s_scr[...]
            sa = _seg_sum(s * spread(at, i), ones2)
            dmb = dmb_scr[...]
            vcol = jnp.dot(jnp.concatenate([dmb * spread(vt_hi, i, BF16), dmb * spread(vt_lo, i, BF16)], axis=1),
                           ones2, preferred_element_type=F32)
            s = s * spread(dt, i) + sa * spread(bt, i) + vcol * spread(kt, i)
            s_scr[...] = s
            yd = _seg_sum(s * spread(rt, i), ones2) * dm
            for q in range(nb * RWKV_PAIRS):
                yrow = jnp.sum(yd[q * RWKV_HEAD_DIM:(q + 1) * RWKV_HEAD_DIM], axis=0, keepdims=True)
                ys[q] = _put_row(ys[q], yrow, i)
        for bb in range(nb):
            for p in range(RWKV_PAIRS):
                y_ref[bb, p, pl.ds(g0, grp), :] = ys[bb * RWKV_PAIRS + p]
        return carry

    lax.fori_loop(0, tc // grp, group, 0)

    @pl.when(t == nt - 1)
    def _():
        sf_ref[...] = s_scr[...].reshape(nb, RWKV_PAIRS, RWKV_HEAD_DIM, LANES)


def _pair_state(s):
    b = s.shape[0]
    s = s.reshape(b, RWKV_PAIRS, 2, RWKV_HEAD_DIM, RWKV_HEAD_DIM)
    return jnp.transpose(s, (0, 1, 3, 2, 4)).reshape(b, RWKV_PAIRS, RWKV_HEAD_DIM, LANES)


def _unpair_state(s):
    b = s.shape[0]
    s = s.reshape(b, RWKV_PAIRS, RWKV_HEAD_DIM, 2, RWKV_HEAD_DIM)
    return jnp.transpose(s, (0, 1, 3, 2, 4)).reshape(b, RWKV_HEADS, RWKV_HEAD_DIM, RWKV_HEAD_DIM)


def _rwkv_scan(seqs, wkv_prev):
    r, d, k, v, a, bv = seqs
    b, t_len, _ = r.shape
    nb = 2
    tc = min(t_len, 256)
    row = pl.BlockSpec((nb, tc, BR_W), lambda i, t: (i, t, 0))
    st = pl.BlockSpec((nb, RWKV_PAIRS, RWKV_HEAD_DIM, LANES), lambda i, t: (i, 0, 0, 0))
    rows = nb * RWKV_PAIRS * RWKV_HEAD_DIM
    y, sf = pl.pallas_call(
        functools.partial(_rwkv_scan_kernel, nb=nb, tc=tc),
        grid=(b // nb, t_len // tc),
        in_specs=[row] * 6 + [st, pl.BlockSpec((2 * LANES, LANES), lambda i, t: (0, 0))],
        out_specs=[pl.BlockSpec((nb, RWKV_PAIRS, tc, LANES), lambda i, t: (i, 0, t, 0)), st],
        out_shape=[jax.ShapeDtypeStruct((b, RWKV_PAIRS, t_len, LANES), F32),
                   jax.ShapeDtypeStruct((b, RWKV_PAIRS, RWKV_HEAD_DIM, LANES), F32)],
        scratch_shapes=[pltpu.VMEM((rows, LANES), F32), pltpu.VMEM((rows, LANES), F32),
                        pltpu.VMEM((rows, LANES), BF16)],
        compiler_params=_params(("parallel", "arbitrary")),
        name="rwkv_scan",
    )(r, d, k, v, a, bv, _pair_state(wkv_prev), _head_ones())
    return y, _unpair_state(sf)


def _rwkv_post_kernel(y_ref, bonus_ref, gate_ref, g_ref, b_ref, ones_ref, o_ref):
    ones2 = ones_ref[...]
    inv = 1.0 / RWKV_HEAD_DIM
    for p in range(RWKV_PAIRS):
        sl = slice(p * LANES, (p + 1) * LANES)
        y = y_ref[0, p]
        m = _seg_sum(y, ones2) * inv
        c = y - m
        var = _seg_sum(c * c, ones2) * inv
        out = c * lax.rsqrt(var + RWKV_LN_EPS) * g_ref[:, sl] + b_ref[:, sl] + bonus_ref[0, :, sl]
        o_ref[0, :, sl] = (out * _silu(gate_ref[0, :, sl])).astype(o_ref.dtype)


def _rwkv_post(y, bonus, p3d, lnx_g, lnx_b):
    b, t_len, _ = bonus.shape
    tt = min(t_len, 512)
    vec = pl.BlockSpec((1, BR_W), lambda i, t: (0, 0))
    return pl.pallas_call(
        _rwkv_post_kernel,
        grid=(b, t_len // tt),
        in_specs=[pl.BlockSpec((1, RWKV_PAIRS, tt, LANES), lambda i, t: (i, 0, t, 0)),
                  pl.BlockSpec((1, tt, BR_W), lambda i, t: (i, t, 0)),
                  _row_spec(tt, 'rwkv_gate'),
                  vec, vec,
                  pl.BlockSpec((2 * LANES, LANES), lambda i, t: (0, 0))],
        out_specs=pl.BlockSpec((1, tt, BR_W), lambda i, t: (i, t, 0)),
        out_shape=jax.ShapeDtypeStruct((b, t_len, BR_W), _branch_dtype(tt)),
        compiler_params=_params(("parallel", "parallel")),
        name="rwkv_post",
    )(y, bonus, p3d, lnx_g.reshape(1, BR_W), lnx_b.reshape(1, BR_W), _head_ones())


def _rope_tables(pos):
    def tab(half):
        freqs = ROPE_THETA ** (-jnp.arange(half, dtype=F32) / half)
        ang = pos.astype(F32)[:, None] * freqs[None, :]
        c, s = jnp.cos(ang), jnp.sin(ang)
        reps = LANES // (2 * half)
        return (jnp.tile(jnp.concatenate([c, c], axis=1), (1, reps)),
                jnp.tile(jnp.concatenate([-s, s], axis=1), (1, reps)))
    c128, s128 = tab(ATTN_HEAD_DIM // 2)
    c64, s64 = tab(IDX_DIM // 2)
    return c128, s128, c64, s64


def _rot128(x, cos, sin):
    return x * cos + pltpu.roll(x, ATTN_HEAD_DIM // 2, 1) * sin


def _rot64(x, cos, sin):
    half = IDX_DIM // 2
    lane = lax.broadcasted_iota(I32, x.shape, 1) % IDX_DIM
    partner = jnp.where(lane < half, pltpu.roll(x, LANES - half, 1), pltpu.roll(x, half, 1))
    return x * cos + partner * sin


def _attn_prep_kernel(q_ref, k_ref, v_ref, iq_ref, sm_ref, qn_ref, kn_ref, c128_ref, s128_ref, c64_ref, s64_ref,
                      qo_ref, kf_ref, kb_ref, vf_ref, vb_ref, iqo_ref, kif_ref, kia_ref, kib_ref, wo_ref):
    c128, s128, c64, s64 = c128_ref[...], s128_ref[...], c64_ref[...], s64_ref[...]
    scale = ATTN_HEAD_DIM ** -0.5
    lowp = qo_ref.dtype
    for h in range(ATTN_HEADS):
        sl = slice(h * LANES, (h + 1) * LANES)
        q = q_ref[0, :, sl]
        q = q * lax.rsqrt(jnp.mean(q * q, axis=-1, keepdims=True) + NORM_EPS) * qn_ref[...]
        qo_ref[0, :, sl] = (_rot128(q, c128, s128) * scale).astype(lowp)
        k = k_ref[0, :, sl]
        k = k * lax.rsqrt(jnp.mean(k * k, axis=-1, keepdims=True) + NORM_EPS) * kn_ref[...]
        k = _rot128(k, c128, s128)
        kf_ref[0, :, sl] = k
        kb_ref[0, :, sl] = k.astype(lowp)
        iqo_ref[0, :, sl] = _rot64(iq_ref[0, :, sl], c64, s64).astype(lowp)
    v = v_ref[0]
    vf_ref[0] = v
    vb_ref[0] = v.astype(lowp)
    sm = sm_ref[0, :, LANES:2 * LANES]
    ki = _rot64(sm, c64, s64)
    lane = lax.broadcasted_iota(I32, ki.shape, 1)
    kif_ref[0] = ki[:, 0:IDX_DIM]
    kz = jnp.where(lane < IDX_DIM, ki, 0.0)
    kia_ref[0] = kz.astype(lowp)
    kib_ref[0] = pltpu.roll(kz, IDX_DIM, 1).astype(lowp)
    wo_ref[0] = sm[:, IDX_DIM:IDX_DIM + IDX_HEADS] * IDX_SCALE


def _attn_prep(p3d, pos, qn, kn):
    b, t_len, _ = p3d.shape
    tt = min(t_len, 256)
    lowp = _branch_dtype(tt)
    c128, s128, c64, s64 = _rope_tables(pos)
    tab = pl.BlockSpec((tt, LANES), lambda i, t: (t, 0))
    hv = pl.BlockSpec((1, LANES), lambda i, t: (0, 0))
    o_row = pl.BlockSpec((1, tt, BR_W), lambda i, t: (i, t, 0))
    o_l = pl.BlockSpec((1, tt, LANES), lambda i, t: (i, t, 0))
    big_f = jax.ShapeDtypeStruct((b, t_len, BR_W), F32)
    big_b = jax.ShapeDtypeStruct((b, t_len, BR_W), lowp)
    outs = pl.pallas_call(
        _attn_prep_kernel,
        grid=(b, t_len // tt),
        in_specs=[_row_spec(tt, 'attn_q'), _row_spec(tt, 'attn_k'), _row_spec(tt, 'attn_v'), _row_spec(tt, 'idx_q'),
                  pl.BlockSpec((1, tt, SMALL_W), lambda i, t: (i, t, SMALL_BLK)),
                  hv, hv, tab, tab, tab, tab],
        out_specs=[o_row, o_row, o_row, o_row, o_row, o_row,
                   pl.BlockSpec((1, tt, IDX_DIM), lambda i, t: (i, t, 0)), o_l, o_l,
                   pl.BlockSpec((1, tt, IDX_HEADS), lambda i, t: (i, t, 0))],
        out_shape=[big_b, big_f, big_b, big_f, big_b, big_b,
                   jax.ShapeDtypeStruct((b, t_len, IDX_DIM), F32),
                   jax.ShapeDtypeStruct((b, t_len, LANES), lowp),
                   jax.ShapeDtypeStruct((b, t_len, LANES), lowp),
                   jax.ShapeDtypeStruct((b, t_len, IDX_HEADS), F32)],
        compiler_params=_params(("parallel", "parallel")),
        name="attn_prep",
    )(p3d, p3d, p3d, p3d, p3d, qn.reshape(1, LANES), kn.reshape(1, LANES), c128, s128, c64, s64)
    return tuple(o if o.dtype == F32 else o.astype(BF16) for o in outs)


def _sort_key(x):
    bits = pltpu.bitcast(x, I32)
    return jnp.where(bits < 0, bits ^ jnp.int32(0x7FFFFFFF), bits)


def _kth_largest_key(load_tile, n_tiles, rows, k_top):
    def count_tile(j, cand):
        x = load_tile(j)
        reps = x.shape[1] // LANES
        hit = jnp.where(x >= jnp.concatenate([cand] * reps, axis=1), 1.0, 0.0)
        part = hit[:, 0:LANES]
        for r in range(1, reps):
            part = part + hit[:, r * LANES:(r + 1) * LANES]
        return part

    def bit_step(i, lo):
        cand = lo + lax.shift_left(jnp.int32(1), jnp.int32(31) - i)
        if isinstance(n_tiles, int):
            acc = count_tile(0, cand)
            for j in range(1, n_tiles):
                acc = acc + count_tile(j, cand)
        else:
            acc = lax.fori_loop(0, n_tiles, lambda j, a: a + count_tile(j, cand),
                                jnp.zeros((rows, LANES), F32))
        cnt = jnp.broadcast_to(jnp.sum(acc, axis=-1, keepdims=True), (rows, LANES))
        return jnp.where(cnt >= k_top, cand, lo)

    lo = lax.fori_loop(0, 32, bit_step, jnp.full((rows, LANES), INT_MIN, I32))
    return jnp.maximum(lo, jnp.int32(INT_MIN + 1))


def _attn_prompt_kernel(q_ref, iq_ref, w_ref, gate_ref, k_ref, v_ref, kia_ref, kib_ref, o_ref,
                        key_scr, m_scr, l_scr, acc_scr, *, k_top, kc):
    qb = pl.program_id(1)
    n_chunks = ((qb + 1) * Q_BLOCK + kc - 1) // kc
    qpos = qb * Q_BLOCK + lax.broadcasted_iota(I32, (Q_BLOCK, kc), 0)
    w = w_ref[0]

    def score_chunk(c, carry):
        k0 = pl.multiple_of(c * kc, kc)
        ka = kia_ref[0, pl.ds(k0, kc), :]
        kb = kib_ref[0, pl.ds(k0, kc), :]
        sc = jnp.zeros((Q_BLOCK, kc), F32)
        for j in range(IDX_HEADS // 2):
            qp = iq_ref[0, :, j * LANES:(j + 1) * LANES]
            for hh, kk in ((0, ka), (1, kb)):
                rel = lax.dot_general(qp, kk, (((1,), (1,)), ((), ())), preferred_element_type=F32)
                h = 2 * j + hh
                sc = sc + jnp.maximum(rel, 0.0) * w[:, h:h + 1]
        spos = k0 + lax.broadcasted_iota(I32, (Q_BLOCK, kc), 1)
        key_scr[:, pl.ds(k0, kc)] = jnp.where(spos <= qpos, _sort_key(sc), jnp.int32(INT_MIN))
        return carry

    lax.fori_loop(0, n_chunks, score_chunk, 0)

    def load_tile(j):
        return key_scr[:, pl.ds(pl.multiple_of(j * kc, kc), kc)]

    reps = kc // LANES
    thr = _kth_largest_key(load_tile, n_chunks, Q_BLOCK, k_top)
    thr_c = jnp.concatenate([thr] * reps, axis=1)

    m_scr[...] = jnp.full(m_scr.shape, NEG_BIG, F32)
    l_scr[...] = jnp.zeros(l_scr.shape, F32)
    acc_scr[...] = jnp.zeros(acc_scr.shape, F32)

    def attend_chunk(c, carry):
        k0 = pl.multiple_of(c * kc, kc)
        sel = key_scr[:, pl.ds(k0, kc)] >= thr_c
        for h in range(ATTN_HEADS):
            sl = slice(h * LANES, (h + 1) * LANES)
            kh = k_ref[0, pl.ds(k0, kc), sl]
            vh = v_ref[0, pl.ds(k0, kc), sl]
            s = lax.dot_general(q_ref[0, :, sl], kh, (((1,), (1,)), ((), ())), preferred_element_type=F32)
            s = jnp.where(sel, s, NEG_BIG)
            m = m_scr[:, sl]
            m_new = jnp.maximum(m, jnp.broadcast_to(jnp.max(s, axis=-1, keepdims=True), (Q_BLOCK, LANES)))
            alpha = jnp.exp(m - m_new)
            p = jnp.exp(s - jnp.concatenate([m_new] * reps, axis=1))
            l_scr[:, sl] = l_scr[:, sl] * alpha + jnp.broadcast_to(jnp.sum(p, axis=-1, keepdims=True),
                                                                    (Q_BLOCK, LANES))
            acc_scr[:, sl] = acc_scr[:, sl] * alpha + jnp.dot(p.astype(BF16), vh, preferred_element_type=F32)
            m_scr[:, sl] = m_new
        return carry

    lax.fori_loop(0, n_chunks, attend_chunk, 0)
    o_ref[0] = (acc_scr[...] / l_scr[...] * _silu(gate_ref[0])).astype(BF16)


def _attn_prompt(prep, p3d):
    qb16, _, kb16, _, vb16, iqb16, _, kia, kib, w = prep
    b, t_len, _ = qb16.shape
    k_top = min(TOPK_MAX, t_len // 4)
    kc = min(256, t_len)
    assert t_len % kc == 0 and kc % Q_BLOCK == 0
    qrow = pl.BlockSpec((1, Q_BLOCK, BR_W), lambda i, t: (i, t, 0))

    def full(width):
        return pl.BlockSpec((1, t_len, width), lambda i, t: (i, 0, 0))

    return pl.pallas_call(
        functools.partial(_attn_prompt_kernel, k_top=k_top, kc=kc),
        grid=(b, t_len // Q_BLOCK),
        in_specs=[qrow, qrow,
                  pl.BlockSpec((1, Q_BLOCK, IDX_HEADS), lambda i, t: (i, t, 0)),
                  _row_spec(Q_BLOCK, 'attn_gate'),
                  full(BR_W), full(BR_W), full(LANES), full(LANES)],
        out_specs=pl.BlockSpec((1, Q_BLOCK, BR_W), lambda i, t: (i, t, 0)),
        out_shape=jax.ShapeDtypeStruct((b, t_len, BR_W), BF16),
        scratch_shapes=[pltpu.VMEM((Q_BLOCK, t_len), I32)] + [pltpu.VMEM((Q_BLOCK, BR_W), F32)] * 3,
        compiler_params=_params(("parallel", "arbitrary")),
        name="attn_prompt",
    )(qb16, iqb16, w, p3d, kb16, vb16, kia, kib)


SCORE_PAGES = 8
ATTEND_PAGES = 4


def _score_tile(qs, w, ki, t_new, new_rows):
    rel = lax.dot_general(qs, ki, (((1,), (1,)), ((), ())), preferred_element_type=F32)
    wr = jnp.maximum(rel, 0.0) * w
    sc = jnp.zeros((SUBLANES, PAGE_SIZE), F32)
    for t in range(t_new):
        sc = _put_row(sc, jnp.sum(wr[t * IDX_HEADS:(t + 1) * IDX_HEADS], axis=0, keepdims=True), t)
    qi = lax.broadcasted_iota(I32, sc.shape, 0)
    si = lax.broadcasted_iota(I32, sc.shape, 1)
    visible = qi < t_new
    if new_rows:
        visible = jnp.logical_and(visible, si <= qi)
    return jnp.where(visible, _sort_key(sc), jnp.int32(INT_MIN))


def _sample_scores_kernel(pt_ref, qs_ref, w_ref, *rest, t_new):
    pages = rest[:SCORE_PAGES]
    new_ref, o_ref, onew_ref = rest[SCORE_PAGES:]
    qs = qs_ref[0]
    w = w_ref[0]
    for j, page_ref in enumerate(pages):
        o_ref[0, j] = _score_tile(qs, w, page_ref[0, 0].astype(BF16), t_new, False)

    @pl.when(pl.program_id(1) == pl.num_programs(1) - 1)
    def _():
        onew_ref[0] = _score_tile(qs, w, new_ref[0], t_new, True)


def _sample_thresh_kernel(key_ref, knew_ref, o_ref, *, n_pages, k_top):
    def load_tile(j):
        return knew_ref[0] if j == n_pages else key_ref[0, j]

    o_ref[0] = _kth_largest_key(load_tile, n_pages + 1, SUBLANES, k_top)


def _sample_attend_kernel(pt_ref, q_ref, key_ref, knew_ref, thr_ref, gate_ref, exp_ref, *rest, t_new):
    kps = rest[:ATTEND_PAGES]
    vps = rest[ATTEND_PAGES:2 * ATTEND_PAGES]
    kn_ref, vn_ref, o_ref, m_scr, l_scr, acc_scr = rest[2 * ATTEND_PAGES:]
    p = pl.program_id(1)
    rows = t_new * ATTN_HEADS
    cols = PAGE_SIZE * ATTN_HEADS

    @pl.when(p == 0)
    def _():
        m_scr[...] = jnp.full(m_scr.shape, NEG_BIG, F32)
        l_scr[...] = jnp.zeros(l_scr.shape, F32)
        acc_scr[...] = jnp.zeros(acc_scr.shape, F32)

    q = q_ref[0]
    thr = thr_ref[0]
    own_head = (lax.broadcasted_iota(I32, (rows, cols), 0) % ATTN_HEADS
                == lax.broadcasted_iota(I32, (rows, cols), 1) % ATTN_HEADS)

    def update(keys, kp, vp):
        s = lax.dot_general(q, kp, (((1,), (1,)), ((), ())), preferred_element_type=F32)
        sel = jnp.where(keys >= thr, 1.0, 0.0)
        sel = jnp.concatenate([jnp.broadcast_to(sel[t:t + 1, :], (ATTN_HEADS, PAGE_SIZE))
                               for t in range(t_new)], axis=0)
        sel = jnp.dot(sel.astype(BF16), exp_ref[...], preferred_element_type=F32)
        s = jnp.where(jnp.logical_and(sel > 0.5, own_head), s, NEG_BIG)
        m = m_scr[...]
        m_new = jnp.maximum(m, jnp.max(s, axis=-1, keepdims=True))
        alpha = jnp.exp(m - m_new)
        pr = jnp.exp(s - m_new)
        l_scr[...] = l_scr[...] * alpha + jnp.sum(pr, axis=-1, keepdims=True)
        acc_scr[...] = acc_scr[...] * alpha + jnp.dot(pr.astype(BF16), vp, preferred_element_type=F32)
        m_scr[...] = m_new

    for j in range(ATTEND_PAGES):
        update(key_ref[0, j], kps[j][0, 0].astype(BF16), vps[j][0, 0].astype(BF16))

    @pl.when(p == pl.num_programs(1) - 1)
    def _():
        update(knew_ref[0], kn_ref[0], vn_ref[0])
        o_ref[0] = acc_scr[...] / l_scr[...] * _silu(gate_ref[0])


def _attn_sample(prep, p3d, cache_k, cache_v, cache_kidx, page_table, layer):
    qb16, _, kb16, _, vb16, iqb16, kif, _, _, w = prep
    b, t_new, _ = qb16.shape
    assert t_new <= SUBLANES
    n_pages = page_table.shape[1]
    assert n_pages % SCORE_PAGES == 0 and n_pages % ATTEND_PAGES == 0
    past = n_pages * PAGE_SIZE
    k_top = min(TOPK_MAX, (past + t_new) // 4)
    n_pool = cache_k.shape[0]
    cols = PAGE_SIZE * ATTN_HEADS
    rows = t_new * ATTN_HEADS
    ck = cache_k.reshape(n_pool, DEPTH, cols, ATTN_HEAD_DIM)
    cv = cache_v.reshape(n_pool, DEPTH, cols, ATTN_HEAD_DIM)
    n_qrow = t_new * IDX_HEADS
    qs = iqb16.reshape(b, n_qrow, IDX_DIM)
    w128 = jnp.broadcast_to(w.reshape(b, n_qrow, 1), (b, n_qrow, PAGE_SIZE))
    ki_new = jnp.pad(kif, ((0, 0), (0, PAGE_SIZE - t_new), (0, 0))).astype(BF16)

    def page(j, per_step):
        return lambda i, p, pt: (pt[i, p * per_step + j], layer, 0, 0)

    keys, keys_new = pl.pallas_call(
        functools.partial(_sample_scores_kernel, t_new=t_new),
        grid_spec=pltpu.PrefetchScalarGridSpec(
            num_scalar_prefetch=1,
            grid=(b, n_pages // SCORE_PAGES),
            in_specs=[pl.BlockSpec((1, n_qrow, IDX_DIM), lambda i, p, pt: (i, 0, 0)),
                      pl.BlockSpec((1, n_qrow, PAGE_SIZE), lambda i, p, pt: (i, 0, 0))]
                     + [pl.BlockSpec((1, 1, PAGE_SIZE, IDX_DIM), page(j, SCORE_PAGES)) for j in range(SCORE_PAGES)]
                     + [pl.BlockSpec((1, PAGE_SIZE, IDX_DIM), lambda i, p, pt: (i, 0, 0))],
            out_specs=[pl.BlockSpec((1, SCORE_PAGES, SUBLANES, PAGE_SIZE), lambda i, p, pt: (i, p, 0, 0)),
                       pl.BlockSpec((1, SUBLANES, PAGE_SIZE), lambda i, p, pt: (i, 0, 0))]),
        out_shape=[jax.ShapeDtypeStruct((b, n_pages, SUBLANES, PAGE_SIZE), I32),
                   jax.ShapeDtypeStruct((b, SUBLANES, PAGE_SIZE), I32)],
        compiler_params=_params(("parallel", "arbitrary")),
        name="sample_scores",
    )(page_table, qs, w128, *([cache_kidx] * SCORE_PAGES), ki_new)

    thr = pl.pallas_call(
        functools.partial(_sample_thresh_kernel, n_pages=n_pages, k_top=k_top),
        grid=(b,),
        in_specs=[pl.BlockSpec((1, n_pages, SUBLANES, PAGE_SIZE), lambda i: (i, 0, 0, 0)),
                  pl.BlockSpec((1, SUBLANES, PAGE_SIZE), lambda i: (i, 0, 0))],
        out_specs=pl.BlockSpec((1, SUBLANES, LANES), lambda i: (i, 0, 0)),
        out_shape=jax.ShapeDtypeStruct((b, SUBLANES, LANES), I32),
        compiler_params=_params(("parallel",)),
        name="sample_thresh",
    )(keys, keys_new)

    expand = jnp.asarray(np.arange(PAGE_SIZE)[:, None] == np.arange(cols)[None, :] // ATTN_HEADS, dtype=BF16)
    pad_new = lambda x: jnp.pad(x.reshape(b, rows, ATTN_HEAD_DIM), ((0, 0), (0, cols - rows), (0, 0)))
    gate = _seg(p3d, 'attn_gate').reshape(b, rows, ATTN_HEAD_DIM)
    whole = lambda r, c: pl.BlockSpec((1, r, c), lambda i, p, pt: (i, 0, 0))
    kv_page = lambda j: pl.BlockSpec((1, 1, cols, ATTN_HEAD_DIM), page(j, ATTEND_PAGES))
    out = pl.pallas_call(
        functools.partial(_sample_attend_kernel, t_new=t_new),
        grid_spec=pltpu.PrefetchScalarGridSpec(
            num_scalar_prefetch=1,
            grid=(b, n_pages // ATTEND_PAGES),
            in_specs=[whole(rows, ATTN_HEAD_DIM),
                      pl.BlockSpec((1, ATTEND_PAGES, SUBLANES, PAGE_SIZE), lambda i, p, pt: (i, p, 0, 0)),
                      whole(SUBLANES, PAGE_SIZE), whole(SUBLANES, LANES), whole(rows, ATTN_HEAD_DIM),
                      pl.BlockSpec((PAGE_SIZE, cols), lambda i, p, pt: (0, 0))]
                     + [kv_page(j) for j in range(ATTEND_PAGES)] + [kv_page(j) for j in range(ATTEND_PAGES)]
                     + [whole(cols, ATTN_HEAD_DIM), whole(cols, ATTN_HEAD_DIM)],
            out_specs=whole(rows, ATTN_HEAD_DIM),
            scratch_shapes=[pltpu.VMEM((rows, 1), F32), pltpu.VMEM((rows, 1), F32),
                            pltpu.VMEM((rows, ATTN_HEAD_DIM), F32)]),
        out_shape=jax.ShapeDtypeStruct((b, rows, ATTN_HEAD_DIM), F32),
        compiler_params=_params(("parallel", "arbitrary")),
        name="sample_attend",
    )(page_table, qb16.reshape(b, rows, ATTN_HEAD_DIM), keys, keys_new, thr, gate, expand,
      *([ck] * ATTEND_PAGES), *([cv] * ATTEND_PAGES), pad_new(kb16), pad_new(vb16))
    return out.reshape(b, t_new, BR_W)


def _layer(x, lw, pos0, pool_prev, shift_prev, wkv_prev, paged):
    b, t_len, _ = x.shape
    x2d = x.reshape(b * t_len, D_MODEL)
    pos = pos0 + jnp.arange(t_len, dtype=I32)
    p2d = _inproj(x2d, lw['norm_g'], lw['w_in'])
    p3d = p2d.reshape(b, t_len, IN_COLS_PAD)

    pool_br, pool_state = _pool(p3d, pool_prev, lw['pool_w'], lw['pool_scale'], pos0)
    gmlp_br, gmlp_vn = _gmlp(p3d, lw['gmlp_ln_g'], lw['gmlp_ln_b'], lw['gmlp_wm'], lw['gmlp_bsb'],
                             emit_vn=paged is not None)
    seqs, shift_state = _rwkv_prep(p3d, shift_prev, lw)
    y, wkv_state = _rwkv_scan(seqs[:6], wkv_prev)
    rwkv_br = _rwkv_post(y, seqs[6], p3d, lw['rwkv_lnx_g'], lw['rwkv_lnx_b'])
    prep = _attn_prep(p3d, pos, lw['attn_qn'], lw['attn_kn'])
    if paged is None:
        attn_br = _attn_prompt(prep, p3d)
    else:
        attn_br = _attn_sample(prep, p3d, *paged)

    flat = lambda t: t.reshape(b * t_len, BR_W).astype(BF16)
    merged = _merge([flat(pool_br), flat(gmlp_br), flat(rwkv_br), flat(attn_br)], lw['w_branch'], p2d)
    y_out = _outproj(x2d, merged, lw['w_out']).reshape(b, t_len, D_MODEL)
    hd = lambda t: t.reshape(b, t_len, ATTN_HEADS, ATTN_HEAD_DIM)
    return y_out, (hd(prep[1]), hd(prep[3]), prep[6], pool_state, shift_state, wkv_state, gmlp_vn)


def _lora_weights(w2, a2):
    z = jnp.zeros_like(w2)
    return jnp.concatenate([w2, z], axis=0).astype(BF16), jnp.concatenate([z, a2], axis=0).astype(BF16)


def kernel(x_prompt, x_sample, cache_k, cache_v, cache_kidx, page_table, state_pool, state_shift, state_wkv,
           norm_g, w_in, pool_w, pool_scale, gmlp_ln_g, gmlp_ln_b, gmlp_ws, gmlp_bs, rwkv_mu, rwkv_w0, rwkv_w2,
           rwkv_a0, rwkv_a2, rwkv_kk, rwkv_ka, rwkv_rk, rwkv_lnx_g, rwkv_lnx_b, attn_qn, attn_kn, w_branch, w_out):
    past = page_table.shape[1] * PAGE_SIZE
    bp = x_prompt.shape[0]
    w_in_p = _permute_w_in(w_in)
    w_branch_b = w_branch.astype(BF16)
    w_out_b = w_out.astype(BF16)
    pool_w_b = pool_w.astype(BF16)
    causal = jnp.tril(jnp.ones((GMLP_CHUNK, GMLP_CHUNK), dtype=bool))
    gmlp_wm = jnp.where(causal[None, None], gmlp_ws, 0.0).astype(BF16)
    gmlp_bsb = jnp.broadcast_to(gmlp_bs[..., None], gmlp_bs.shape + (GMLP_GROUP_W,))
    xp, xs = x_prompt, x_sample
    outs_p, outs_s = [], []
    for l in range(DEPTH):
        wl, al = _lora_weights(rwkv_w2[l], rwkv_a2[l])
        lw = {
            'norm_g': norm_g[l], 'w_in': w_in_p[l], 'pool_w': pool_w_b[l], 'pool_scale': pool_scale[l],
            'gmlp_ln_g': gmlp_ln_g[l], 'gmlp_ln_b': gmlp_ln_b[l], 'gmlp_wm': gmlp_wm[l], 'gmlp_bsb': gmlp_bsb[l],
            'rwkv_mu': rwkv_mu[l], 'rwkv_w0': rwkv_w0[l], 'rwkv_wl': wl, 'rwkv_a0': rwkv_a0[l],
            'rwkv_al': al, 'rwkv_kk': rwkv_kk[l], 'rwkv_ka': rwkv_ka[l], 'rwkv_rk': rwkv_rk[l].reshape(BR_W),
            'rwkv_lnx_g': rwkv_lnx_g[l], 'rwkv_lnx_b': rwkv_lnx_b[l], 'attn_qn': attn_qn[l],
            'attn_kn': attn_kn[l], 'w_branch': w_branch_b[l], 'w_out': w_out_b[l],
        }
        xp, st_p = _layer(
            xp, lw, 0,
            jnp.zeros((bp, POOL_STATE, BR_W), F32),
            jnp.zeros((bp, RWKV_SHIFT_W), F32),
            jnp.zeros((bp, RWKV_HEADS, RWKV_HEAD_DIM, RWKV_HEAD_DIM), F32),
            None)
        xs, st_s = _layer(xs, lw, past, state_pool[:, l], state_shift[:, l], state_wkv[:, l],
                          (cache_k, cache_v, cache_kidx, page_table, l))
        outs_p.append(st_p)
        outs_s.append(st_s)
    stk = lambda outs, i: jnp.stack([o[i] for o in outs], axis=1)
    k_p, v_p, ki_p = stk(outs_p, 0), stk(outs_p, 1), stk(outs_p, 2)
    pool_p, shift_p, wkv_p = stk(outs_p, 3), stk(outs_p, 4), stk(outs_p, 5)
    k_s, v_s, ki_s = stk(outs_s, 0), stk(outs_s, 1), stk(outs_s, 2)
    pool_s, shift_s, wkv_s = stk(outs_s, 3), stk(outs_s, 4), stk(outs_s, 5)
    gmlp_v_s = stk(outs_s, 6)
    return (xp, xs, k_p, v_p, ki_p, k_s, v_s, ki_s, pool_p, pool_s, shift_p, shift_s, wkv_p, wkv_s, gmlp_v_s)
```

```python
import functools

import numpy as np
import jax
import jax.numpy as jnp
from jax import lax
from jax.experimental import pallas as pl
from jax.experimental.pallas import tpu as pltpu

F32 = jnp.float32
BF16 = jnp.bfloat16
I32 = jnp.int32

D_MODEL = 2048
DEPTH = 4
PAGE_SIZE = 128
BR_W = D_MODEL // 2
N_BRANCH = 4
POOL_WINDOWS = (2, 4, 8, 16)
POOL_GROUPS = 4
POOL_GROUP_W = BR_W // POOL_GROUPS
POOL_STATE = 15
POOL_HALO = 16
GMLP_CHUNK = 128
GMLP_GROUPS = 8
GMLP_GROUP_W = BR_W // GMLP_GROUPS
GMLP_LN_EPS = 1e-5
RWKV_HEAD_DIM = 64
RWKV_HEADS = BR_W // RWKV_HEAD_DIM
RWKV_PAIRS = RWKV_HEADS // 2
RWKV_W_LORA = 64
RWKV_A_LORA = 64
RWKV_SHIFT_W = 3 * BR_W + RWKV_W_LORA + RWKV_A_LORA
RWKV_LN_EPS = 64e-5
ATTN_HEAD_DIM = 128
ATTN_HEADS = BR_W // ATTN_HEAD_DIM
IDX_HEADS = 16
IDX_DIM = 64
IDX_SCALE = (IDX_HEADS * IDX_DIM) ** -0.5
TOPK_MAX = 256
Q_BLOCK = 128
ROPE_THETA = 10000.0
NORM_EPS = 1e-6
LANES = 128
SUBLANES = 8
BF16_ROWS = 16
INT_MIN = -2 ** 31
NEG_BIG = -1e30

SEG_SRC = (
    ('pool_u', BR_W), ('pool_gate', BR_W),
    ('gmlp_u', BR_W), ('gmlp_v', BR_W), ('gmlp_gate', BR_W),
    ('rwkv_shift', RWKV_SHIFT_W), ('rwkv_gate', BR_W),
    ('attn_q', BR_W), ('attn_k', BR_W), ('attn_v', BR_W),
    ('idx_q', IDX_HEADS * IDX_DIM), ('idx_k', IDX_DIM), ('idx_w', IDX_HEADS),
    ('attn_gate', BR_W), ('merge', N_BRANCH * D_MODEL),
)
SEG_DST_ORDER = ('pool_u', 'pool_gate', 'gmlp_u', 'gmlp_v', 'gmlp_gate', 'rwkv_gate',
                 'attn_q', 'attn_k', 'attn_v', 'idx_q', 'attn_gate', 'merge',
                 'rwkv_shift', 'idx_k', 'idx_w')
IN_COLS = sum(w for _, w in SEG_SRC)
IN_COLS_PAD = 23040
VMEM_LIMIT = 56 * 1024 * 1024


def _seg_offsets():
    src, o = {}, 0
    for name, w in SEG_SRC:
        src[name] = (o, w)
        o += w
    dst, o = {}, 0
    for name in SEG_DST_ORDER:
        dst[name] = o
        o += src[name][1]
    return src, dst


SRC_OFF, DST_OFF = _seg_offsets()
SMALL_W = 256
SMALL_BLK = (DST_OFF['rwkv_shift'] + 3 * BR_W) // SMALL_W
assert (DST_OFF['rwkv_shift'] + 3 * BR_W) % SMALL_W == 0
assert DST_OFF['idx_k'] == SMALL_BLK * SMALL_W + 128 and DST_OFF['idx_w'] == SMALL_BLK * SMALL_W + 192
assert RWKV_W_LORA + RWKV_A_LORA == LANES


def _cblk(name):
    assert DST_OFF[name] % BR_W == 0
    return DST_OFF[name] // BR_W


def _permute_w_in(w_in):
    parts = [w_in[..., SRC_OFF[n][0]:SRC_OFF[n][0] + SRC_OFF[n][1]] for n in SEG_DST_ORDER]
    parts.append(jnp.zeros(w_in.shape[:-1] + (IN_COLS_PAD - IN_COLS,), w_in.dtype))
    return jnp.concatenate(parts, axis=-1).astype(BF16)


def _seg(p, name):
    return p[..., DST_OFF[name]:DST_OFF[name] + SRC_OFF[name][1]]


def _params(sem):
    return pltpu.CompilerParams(dimension_semantics=sem, vmem_limit_bytes=VMEM_LIMIT)


def _branch_dtype(tt):
    return BF16 if tt % BF16_ROWS == 0 else F32


def _row_spec(tt, name):
    c = _cblk(name)
    return pl.BlockSpec((1, tt, BR_W), lambda i, t: (i, t, c))


def _silu(x):
    return x * jax.nn.sigmoid(x)


def _head_ones():
    r = np.arange(2 * LANES)[:, None] % LANES
    c = np.arange(LANES)[None, :]
    return jnp.asarray((r // RWKV_HEAD_DIM) == (c // RWKV_HEAD_DIM), dtype=BF16)


def _seg_sum(x, ones2):
    hi = x.astype(BF16)
    lo = (x - hi.astype(F32)).astype(BF16)
    return jnp.dot(jnp.concatenate([hi, lo], axis=1), ones2, preferred_element_type=F32)


def _put_row(dst, row, i):
    sub = lax.broadcasted_iota(I32, dst.shape, 0)
    return jnp.where(sub == i, jnp.broadcast_to(row, dst.shape), dst)


def _inproj_kernel(x_ref, g_ref, w_ref, o_ref, h_scr):
    @pl.when(pl.program_id(1) == 0)
    def _():
        x = x_ref[...]
        ms = jnp.mean(x * x, axis=-1, keepdims=True)
        h_scr[...] = (x * lax.rsqrt(ms + NORM_EPS) * g_ref[...]).astype(BF16)

    o_ref[...] = jnp.dot(h_scr[...], w_ref[...], preferred_element_type=F32)


def _inproj(x2d, g, w_bf16):
    m = x2d.shape[0]
    tm = min(m, 1024)
    tn = 512
    return pl.pallas_call(
        _inproj_kernel,
        grid=(m // tm, IN_COLS_PAD // tn),
        in_specs=[pl.BlockSpec((tm, D_MODEL), lambda i, j: (i, 0)),
                  pl.BlockSpec((1, D_MODEL), lambda i, j: (0, 0)),
                  pl.BlockSpec((D_MODEL, tn), lambda i, j: (0, j))],
        out_specs=pl.BlockSpec((tm, tn), lambda i, j: (i, j)),
        out_shape=jax.ShapeDtypeStruct((m, IN_COLS_PAD), F32),
        scratch_shapes=[pltpu.VMEM((tm, D_MODEL), BF16)],
        compiler_params=_params(("parallel", "arbitrary")),
        name="inproj",
    )(x2d, g.reshape(1, D_MODEL), w_bf16)


def _merge_kernel(b0, b1, b2, b3, wb_ref, g0, g1, g2, g3, o_ref):
    acc = None
    for n, (b_ref, g_ref) in enumerate(((b0, g0), (b1, g1), (b2, g2), (b3, g3))):
        proj = jnp.dot(b_ref[...], wb_ref[n], preferred_element_type=F32)
        term = jax.nn.sigmoid(g_ref[...]) * proj
        acc = term if acc is None else acc + term
    o_ref[...] = acc.astype(BF16)


def _merge(branches, wb_bf16, p2d):
    m = p2d.shape[0]
    tm = min(m, 512)
    tn = 512
    goff = DST_OFF['merge'] // tn
    gstep = D_MODEL // tn
    bspec = pl.BlockSpec((tm, BR_W), lambda i, j: (i, 0))

    def gate_map(n):
        return lambda i, j: (i, goff + n * gstep + j)

    gspecs = [pl.BlockSpec((tm, tn), gate_map(n)) for n in range(N_BRANCH)]
    return pl.pallas_call(
        _merge_kernel,
        grid=(m // tm, D_MODEL // tn),
        in_specs=[bspec, bspec, bspec, bspec,
                  pl.BlockSpec((N_BRANCH, BR_W, tn), lambda i, j: (0, 0, j))] + gspecs,
        out_specs=pl.BlockSpec((tm, tn), lambda i, j: (i, j)),
        out_shape=jax.ShapeDtypeStruct((m, D_MODEL), BF16),
        compiler_params=_params(("parallel", "arbitrary")),
        name="merge",
    )(*branches, wb_bf16, p2d, p2d, p2d, p2d)


def _outproj_kernel(x_ref, m_ref, w_ref, o_ref):
    o_ref[...] = x_ref[...] + jnp.dot(m_ref[...], w_ref[...], preferred_element_type=F32)


def _outproj(x2d, merged, wo_bf16):
    m = x2d.shape[0]
    tm = min(m, 1024)
    tn = 512
    return pl.pallas_call(
        _outproj_kernel,
        grid=(m // tm, D_MODEL // tn),
        in_specs=[pl.BlockSpec((tm, tn), lambda i, j: (i, j)),
                  pl.BlockSpec((tm, D_MODEL), lambda i, j: (i, 0)),
                  pl.BlockSpec((D_MODEL, tn), lambda i, j: (0, j))],
        out_specs=pl.BlockSpec((tm, tn), lambda i, j: (i, j)),
        out_shape=jax.ShapeDtypeStruct((m, D_MODEL), F32),
        compiler_params=_params(("parallel", "arbitrary")),
        name="outproj",
    )(x2d, merged, wo_bf16)


def _pool_kernel(u_ref, gate_ref, prev_ref, w_ref, scale_ref, o_ref, st_ref, ext, *, tt, pos0):
    t = pl.program_id(1)
    nt = pl.num_programs(1)

    @pl.when(t == 0)
    def _():
        ext[0:1, :] = jnp.zeros((1, BR_W), F32)
        ext[1:POOL_HALO, :] = prev_ref[0]

    @pl.when(t > 0)
    def _():
        ext[0:POOL_HALO, :] = ext[tt:tt + POOL_HALO, :]

    ext[POOL_HALO:POOL_HALO + tt, :] = u_ref[0]
    pos = pos0 + t * tt + lax.broadcasted_iota(I32, (tt, POOL_GROUP_W), 0)
    for g, w in enumerate(POOL_WINDOWS):
        sl = slice(g * POOL_GROUP_W, (g + 1) * POOL_GROUP_W)
        cur = ext[POOL_HALO:POOL_HALO + tt, sl]
        acc = cur
        for i in range(1, w):
            acc = acc + ext[POOL_HALO - i:POOL_HALO - i + tt, sl]
        cnt = jnp.minimum(w, pos + 1).astype(F32)
        pooled = acc / cnt - cur
        mixed = jnp.dot(pooled.astype(BF16), w_ref[g], preferred_element_type=F32)
        o_ref[0, :, sl] = (mixed * scale_ref[:, sl] * _silu(gate_ref[0, :, sl])).astype(o_ref.dtype)

    @pl.when(t == nt - 1)
    def _():
        st_ref[0] = ext[tt + 1:tt + POOL_HALO, :]


def _pool(p3d, prev, pool_w_bf16, scale, pos0):
    b, t_len, _ = p3d.shape
    tt = min(t_len, 512)
    return pl.pallas_call(
        functools.partial(_pool_kernel, tt=tt, pos0=pos0),
        grid=(b, t_len // tt),
        in_specs=[_row_spec(tt, 'pool_u'), _row_spec(tt, 'pool_gate'),
                  pl.BlockSpec((1, POOL_STATE, BR_W), lambda i, t: (i, 0, 0)),
                  pl.BlockSpec((POOL_GROUPS, POOL_GROUP_W, POOL_GROUP_W), lambda i, t: (0, 0, 0)),
                  pl.BlockSpec((1, BR_W), lambda i, t: (0, 0))],
        out_specs=[pl.BlockSpec((1, tt, BR_W), lambda i, t: (i, t, 0)),
                   pl.BlockSpec((1, POOL_STATE, BR_W), lambda i, t: (i, 0, 0))],
        out_shape=[jax.ShapeDtypeStruct((b, t_len, BR_W), _branch_dtype(tt)),
                   jax.ShapeDtypeStruct((b, POOL_STATE, BR_W), F32)],
        scratch_shapes=[pltpu.VMEM((POOL_HALO + tt, BR_W), F32)],
        compiler_params=_params(("parallel", "arbitrary")),
        name="pool",
    )(p3d, p3d, prev, pool_w_bf16, scale.reshape(1, BR_W))


def _gmlp_kernel(u_ref, v_ref, gate_ref, lng_ref, lnb_ref, wm_ref, bsb_ref, o_ref, *rest, tt, emit_vn):
    if emit_vn:
        vn_ref = rest[0]
        rest = rest[1:]
    v = v_ref[0]
    mu = jnp.mean(v, axis=-1, keepdims=True)
    var = jnp.mean(jnp.square(v - mu), axis=-1, keepdims=True)
    vn = (v - mu) * lax.rsqrt(var + GMLP_LN_EPS) * lng_ref[...] + lnb_ref[...]
    if emit_vn:
        vn_ref[0] = vn
    if tt % GMLP_CHUNK == 0:
        rows = GMLP_CHUNK
        n_chunks = tt // GMLP_CHUNK
        vnb = vn.astype(BF16)
    else:
        pad = rest[0]
        rows = tt
        n_chunks = 1
        pad[...] = jnp.zeros_like(pad)
        pad[0:tt, :] = vn
        vnb = pad[...].astype(BF16)
    for c in range(n_chunks):
        for g in range(GMLP_GROUPS):
            sl = slice(g * GMLP_GROUP_W, (g + 1) * GMLP_GROUP_W)
            rs = slice(c * rows, (c + 1) * rows)
            vc = vnb[c * GMLP_CHUNK:(c + 1) * GMLP_CHUNK, sl]
            mixed = jnp.dot(wm_ref[g], vc, preferred_element_type=F32) + bsb_ref[g]
            o_ref[0, rs, sl] = (u_ref[0, rs, sl] * mixed[0:rows] * _silu(gate_ref[0, rs, sl])).astype(o_ref.dtype)


def _gmlp(p3d, ln_g, ln_b, wm_bf16, bsb, emit_vn):
    b, t_len, _ = p3d.shape
    tt = min(t_len, 512)
    assert tt % GMLP_CHUNK == 0 or (tt == t_len and tt < GMLP_CHUNK)
    out_specs = [pl.BlockSpec((1, tt, BR_W), lambda i, t: (i, t, 0))]
    out_shape = [jax.ShapeDtypeStruct((b, t_len, BR_W), _branch_dtype(tt))]
    if emit_vn:
        out_specs.append(pl.BlockSpec((1, tt, BR_W), lambda i, t: (i, t, 0)))
        out_shape.append(jax.ShapeDtypeStruct((b, t_len, BR_W), F32))
    scratch = [] if tt % GMLP_CHUNK == 0 else [pltpu.VMEM((GMLP_CHUNK, BR_W), F32)]
    res = pl.pallas_call(
        functools.partial(_gmlp_kernel, tt=tt, emit_vn=emit_vn),
        grid=(b, t_len // tt),
        in_specs=[_row_spec(tt, 'gmlp_u'), _row_spec(tt, 'gmlp_v'), _row_spec(tt, 'gmlp_gate'),
                  pl.BlockSpec((1, BR_W), lambda i, t: (0, 0)),
                  pl.BlockSpec((1, BR_W), lambda i, t: (0, 0)),
                  pl.BlockSpec((GMLP_GROUPS, GMLP_CHUNK, GMLP_CHUNK), lambda i, t: (0, 0, 0)),
                  pl.BlockSpec((GMLP_GROUPS, GMLP_CHUNK, GMLP_GROUP_W), lambda i, t: (0, 0, 0))],
        out_specs=out_specs,
        out_shape=out_shape,
        scratch_shapes=scratch,
        compiler_params=_params(("parallel", "parallel")),
        name="gmlp",
    )(p3d, p3d, p3d, ln_g.reshape(1, BR_W), ln_b.reshape(1, BR_W), wm_bf16, bsb)
    return (res[0], res[1]) if emit_vn else (res[0], None)


def _rwkv_prep_kernel(r_ref, k_ref, v_ref, sm_ref, prev_ref, mu_ref, w0_ref, a0_ref, wl_ref, al_ref,
                      kk_ref, ka_ref, rk_ref, ones_ref,
                      ro_ref, do_ref, ko_ref, vo_ref, ao_ref, bo_ref, bonus_ref, st_ref,
                      buf, *, tt):
    t = pl.program_id(1)
    nt = pl.num_programs(1)

    @pl.when(t == 0)
    def _():
        buf[7:8, :] = prev_ref[0]

    @pl.when(t > 0)
    def _():
        buf[7:8, :] = buf[7 + tt:8 + tt, :]

    buf[8:8 + tt, 0:BR_W] = r_ref[0]
    buf[8:8 + tt, BR_W:2 * BR_W] = k_ref[0]
    buf[8:8 + tt, 2 * BR_W:3 * BR_W] = v_ref[0]
    buf[8:8 + tt, 3 * BR_W:RWKV_SHIFT_W] = sm_ref[0, :, 0:LANES]

    @pl.when(t == nt - 1)
    def _():
        st_ref[0] = buf[7 + tt:8 + tt, :]

    ones2 = ones_ref[...]

    def mixed(lo, hi):
        cur = buf[8:8 + tt, lo:hi]
        prv = buf[7:7 + tt, lo:hi]
        return cur + (prv - cur) * mu_ref[:, lo:hi]

    zs = mixed(3 * BR_W, RWKV_SHIFT_W)
    w_pre = w0_ref[...] + jnp.dot(jnp.tanh(zs).astype(BF16), wl_ref[...], preferred_element_type=F32)
    w = -jax.nn.softplus(-w_pre) - 0.5
    do_ref[0] = jnp.exp(-jnp.exp(w))
    a = jax.nn.sigmoid(a0_ref[...] + jnp.dot(zs.astype(BF16), al_ref[...], preferred_element_type=F32))
    r = mixed(0, BR_W)
    k = mixed(BR_W, 2 * BR_W)
    v = mixed(2 * BR_W, 3 * BR_W)
    ro_ref[0] = r
    vo_ref[0] = v
    k2 = k * (1.0 + (a - 1.0) * ka_ref[...])
    ko_ref[0] = k2
    kk = k * kk_ref[...]
    rkk = r * k2 * rk_ref[...]
    for j in range(RWKV_PAIRS):
        sl = slice(j * LANES, (j + 1) * LANES)
        kkj = kk[:, sl]
        nrm = jnp.sqrt(_seg_sum(kkj * kkj, ones2))
        kkn = kkj / jnp.maximum(nrm, 1e-12)
        ao_ref[0, :, sl] = -kkn
        bo_ref[0, :, sl] = kkn * a[:, sl]
        bonus_ref[0, :, sl] = _seg_sum(rkk[:, sl], ones2) * v[:, sl]


def _rwkv_prep(p3d, shift_prev, lw):
    b, t_len, _ = p3d.shape
    tt = min(t_len, 256)
    base = _cblk('rwkv_shift')

    def row(c):
        return pl.BlockSpec((1, tt, BR_W), lambda i, t: (i, t, c))

    vec = pl.BlockSpec((1, BR_W), lambda i, t: (0, 0))
    out_row = pl.BlockSpec((1, tt, BR_W), lambda i, t: (i, t, 0))
    big = jax.ShapeDtypeStruct((b, t_len, BR_W), F32)
    outs = pl.pallas_call(
        functools.partial(_rwkv_prep_kernel, tt=tt),
        grid=(b, t_len // tt),
        in_specs=[row(base), row(base + 1), row(base + 2),
                  pl.BlockSpec((1, tt, SMALL_W), lambda i, t: (i, t, SMALL_BLK)),
                  pl.BlockSpec((1, 1, RWKV_SHIFT_W), lambda i, t: (i, 0, 0)),
                  pl.BlockSpec((1, RWKV_SHIFT_W), lambda i, t: (0, 0)),
                  vec, vec,
                  pl.BlockSpec((LANES, BR_W), lambda i, t: (0, 0)),
                  pl.BlockSpec((LANES, BR_W), lambda i, t: (0, 0)),
                  vec, vec, vec,
                  pl.BlockSpec((2 * LANES, LANES), lambda i, t: (0, 0))],
        out_specs=[out_row] * 7 + [pl.BlockSpec((1, 1, RWKV_SHIFT_W), lambda i, t: (i, 0, 0))],
        out_shape=[big] * 7 + [jax.ShapeDtypeStruct((b, 1, RWKV_SHIFT_W), F32)],
        scratch_shapes=[pltpu.VMEM((8 + tt, RWKV_SHIFT_W), F32)],
        compiler_params=_params(("parallel", "arbitrary")),
        name="rwkv_prep",
    )(p3d, p3d, p3d, p3d, shift_prev.reshape(b, 1, RWKV_SHIFT_W), lw['rwkv_mu'].reshape(1, RWKV_SHIFT_W),
      lw['rwkv_w0'].reshape(1, BR_W), lw['rwkv_a0'].reshape(1, BR_W), lw['rwkv_wl'], lw['rwkv_al'],
      lw['rwkv_kk'].reshape(1, BR_W), lw['rwkv_ka'].reshape(1, BR_W), lw['rwkv_rk'].reshape(1, BR_W),
      _head_ones())
    return outs[:7], outs[7].reshape(b, RWKV_SHIFT_W)


def _rwkv_scan_kernel(r_ref, d_ref, k_ref, v_ref, a_ref, b_ref, s0_ref, ones_ref, y_ref, sf_ref,
                      s_scr, dm_scr, dmb_scr, *, nb, tc):
    t = pl.program_id(1)
    nt = pl.num_programs(1)
    rows = nb * RWKV_PAIRS * RWKV_HEAD_DIM
    grp = min(SUBLANES, tc)

    @pl.when(t == 0)
    def _():
        s_scr[...] = s0_ref[...].reshape(rows, LANES)
        ri = lax.broadcasted_iota(I32, (rows, LANES), 0) % RWKV_HEAD_DIM
        ci = lax.broadcasted_iota(I32, (rows, LANES), 1) % RWKV_HEAD_DIM
        dm_scr[...] = jnp.where(ri == ci, 1.0, 0.0)
        dmb_scr[...] = jnp.where(ri == ci, 1.0, 0.0).astype(BF16)

    ones2 = ones_ref[...]
    first_head = lax.broadcasted_iota(I32, (rows, LANES), 1) < RWKV_HEAD_DIM

    def spread(tile, i, dtype=F32):
        pieces = []
        for bb in range(nb):
            for p in range(RWKV_PAIRS):
                row = tile[bb, i:i + 1, p * LANES:(p + 1) * LANES].astype(dtype)
                pieces.append(jnp.broadcast_to(row, (RWKV_HEAD_DIM, LANES)))
        return jnp.concatenate(pieces, axis=0)

    def group(gi, carry):
        g0 = pl.multiple_of(gi * grp, grp)
        rt = r_ref[:, pl.ds(g0, grp), :]
        dt = d_ref[:, pl.ds(g0, grp), :]
        kt = k_ref[:, pl.ds(g0, grp), :]
        vt = v_ref[:, pl.ds(g0, grp), :]
        at = a_ref[:, pl.ds(g0, grp), :]
        bt = b_ref[:, pl.ds(g0, grp), :]
        vt_hi = vt.astype(BF16).astype(F32)
        vt_lo = vt - vt_hi
        ys = [jnp.zeros((grp, LANES), F32) for _ in range(nb * RWKV_PAIRS)]
        for i in range(grp):
            dm = dm_scr[...]
            s = s_scr[...]
            sa = _seg_sum(s * spread(at, i), ones2)
            dmb = dmb_scr[...]
            vcol = jnp.dot(jnp.concatenate([dmb * spread(vt_hi, i, BF16), dmb * spread(vt_lo, i, BF16)], axis=1),
                           ones2, preferred_element_type=F32)
            s = s * spread(dt, i) + sa * spread(bt, i) + vcol * spread(kt, i)
            s_scr[...] = s
            q = s * spread(rt, i)
            y0 = jnp.sum(jnp.where(first_head, q, 0.0), axis=-1, keepdims=True)
            y1 = jnp.sum(jnp.where(first_head, 0.0, q), axis=-1, keepdims=True)
            yd = jnp.where(first_head, y0, y1) * dm
            for q in range(nb * RWKV_PAIRS):
                yrow = jnp.sum(yd[q * RWKV_HEAD_DIM:(q + 1) * RWKV_HEAD_DIM], axis=0, keepdims=True)
                ys[q] = _put_row(ys[q], yrow, i)
        for bb in range(nb):
            for p in range(RWKV_PAIRS):
                y_ref[bb, p, pl.ds(g0, grp), :] = ys[bb * RWKV_PAIRS + p]
        return carry

    lax.fori_loop(0, tc // grp, group, 0)

    @pl.when(t == nt - 1)
    def _():
        sf_ref[...] = s_scr[...].reshape(nb, RWKV_PAIRS, RWKV_HEAD_DIM, LANES)


def _pair_state(s):
    b = s.shape[0]
    s = s.reshape(b, RWKV_PAIRS, 2, RWKV_HEAD_DIM, RWKV_HEAD_DIM)
    return jnp.transpose(s, (0, 1, 3, 2, 4)).reshape(b, RWKV_PAIRS, RWKV_HEAD_DIM, LANES)


def _unpair_state(s):
    b = s.shape[0]
    s = s.reshape(b, RWKV_PAIRS, RWKV_HEAD_DIM, 2, RWKV_HEAD_DIM)
    return jnp.transpose(s, (0, 1, 3, 2, 4)).reshape(b, RWKV_HEADS, RWKV_HEAD_DIM, RWKV_HEAD_DIM)


def _rwkv_scan(seqs, wkv_prev):
    r, d, k, v, a, bv = seqs
    b, t_len, _ = r.shape
    nb = 2
    tc = min(t_len, 256)
    row = pl.BlockSpec((nb, tc, BR_W), lambda i, t: (i, t, 0))
    st = pl.BlockSpec((nb, RWKV_PAIRS, RWKV_HEAD_DIM, LANES), lambda i, t: (i, 0, 0, 0))
    rows = nb * RWKV_PAIRS * RWKV_HEAD_DIM
    y, sf = pl.pallas_call(
        functools.partial(_rwkv_scan_kernel, nb=nb, tc=tc),
        grid=(b // nb, t_len // tc),
        in_specs=[row] * 6 + [st, pl.BlockSpec((2 * LANES, LANES), lambda i, t: (0, 0))],
        out_specs=[pl.BlockSpec((nb, RWKV_PAIRS, tc, LANES), lambda i, t: (i, 0, t, 0)), st],
        out_shape=[jax.ShapeDtypeStruct((b, RWKV_PAIRS, t_len, LANES), F32),
                   jax.ShapeDtypeStruct((b, RWKV_PAIRS, RWKV_HEAD_DIM, LANES), F32)],
        scratch_shapes=[pltpu.VMEM((rows, LANES), F32), pltpu.VMEM((rows, LANES), F32),
                        pltpu.VMEM((rows, LANES), BF16)],
        compiler_params=_params(("parallel", "arbitrary")),
        name="rwkv_scan",
    )(r, d, k, v, a, bv, _pair_state(wkv_prev), _head_ones())
    return y, _unpair_state(sf)


def _rwkv_post_kernel(y_ref, bonus_ref, gate_ref, g_ref, b_ref, ones_ref, o_ref):
    ones2 = ones_ref[...]
    inv = 1.0 / RWKV_HEAD_DIM
    for p in range(RWKV_PAIRS):
        sl = slice(p * LANES, (p + 1) * LANES)
        y = y_ref[0, p]
        m = _seg_sum(y, ones2) * inv
        c = y - m
        var = _seg_sum(c * c, ones2) * inv
        out = c * lax.rsqrt(var + RWKV_LN_EPS) * g_ref[:, sl] + b_ref[:, sl] + bonus_ref[0, :, sl]
        o_ref[0, :, sl] = (out * _silu(gate_ref[0, :, sl])).astype(o_ref.dtype)


def _rwkv_post(y, bonus, p3d, lnx_g, lnx_b):
    b, t_len, _ = bonus.shape
    tt = min(t_len, 512)
    vec = pl.BlockSpec((1, BR_W), lambda i, t: (0, 0))
    return pl.pallas_call(
        _rwkv_post_kernel,
        grid=(b, t_len // tt),
        in_specs=[pl.BlockSpec((1, RWKV_PAIRS, tt, LANES), lambda i, t: (i, 0, t, 0)),
                  pl.BlockSpec((1, tt, BR_W), lambda i, t: (i, t, 0)),
                  _row_spec(tt, 'rwkv_gate'),
                  vec, vec,
                  pl.BlockSpec((2 * LANES, LANES), lambda i, t: (0, 0))],
        out_specs=pl.BlockSpec((1, tt, BR_W), lambda i, t: (i, t, 0)),
        out_shape=jax.ShapeDtypeStruct((b, t_len, BR_W), _branch_dtype(tt)),
        compiler_params=_params(("parallel", "parallel")),
        name="rwkv_post",
    )(y, bonus, p3d, lnx_g.reshape(1, BR_W), lnx_b.reshape(1, BR_W), _head_ones())


def _rope_tables(pos):
    def tab(half):
        freqs = ROPE_THETA ** (-jnp.arange(half, dtype=F32) / half)
        ang = pos.astype(F32)[:, None] * freqs[None, :]
        c, s = jnp.cos(ang), jnp.sin(ang)
        reps = LANES // (2 * half)
        return (jnp.tile(jnp.concatenate([c, c], axis=1), (1, reps)),
                jnp.tile(jnp.concatenate([-s, s], axis=1), (1, reps)))
    c128, s128 = tab(ATTN_HEAD_DIM // 2)
    c64, s64 = tab(IDX_DIM // 2)
    return c128, s128, c64, s64


def _rot128(x, cos, sin):
    return x * cos + pltpu.roll(x, ATTN_HEAD_DIM // 2, 1) * sin


def _rot64(x, cos, sin):
    half = IDX_DIM // 2
    lane = lax.broadcasted_iota(I32, x.shape, 1) % IDX_DIM
    partner = jnp.where(lane < half, pltpu.roll(x, LANES - half, 1), pltpu.roll(x, half, 1))
    return x * cos + partner * sin


def _attn_prep_kernel(q_ref, k_ref, v_ref, iq_ref, sm_ref, qn_ref, kn_ref, c128_ref, s128_ref, c64_ref, s64_ref,
                      qo_ref, kf_ref, kb_ref, vf_ref, vb_ref, iqo_ref, kif_ref, kia_ref, kib_ref, wo_ref):
    c128, s128, c64, s64 = c128_ref[...], s128_ref[...], c64_ref[...], s64_ref[...]
    scale = ATTN_HEAD_DIM ** -0.5
    lowp = qo_ref.dtype
    for h in range(ATTN_HEADS):
        sl = slice(h * LANES, (h + 1) * LANES)
        q = q_ref[0, :, sl]
        q = q * lax.rsqrt(jnp.mean(q * q, axis=-1, keepdims=True) + NORM_EPS) * qn_ref[...]
        qo_ref[0, :, sl] = (_rot128(q, c128, s128) * scale).astype(lowp)
        k = k_ref[0, :, sl]
        k = k * lax.rsqrt(jnp.mean(k * k, axis=-1, keepdims=True) + NORM_EPS) * kn_ref[...]
        k = _rot128(k, c128, s128)
        kf_ref[0, :, sl] = k
        kb_ref[0, :, sl] = k.astype(lowp)
        iqo_ref[0, :, sl] = _rot64(iq_ref[0, :, sl], c64, s64).astype(lowp)
    v = v_ref[0]
    vf_ref[0] = v
    vb_ref[0] = v.astype(lowp)
    sm = sm_ref[0, :, LANES:2 * LANES]
    ki = _rot64(sm, c64, s64)
    lane = lax.broadcasted_iota(I32, ki.shape, 1)
    kif_ref[0] = ki[:, 0:IDX_DIM]
    kz = jnp.where(lane < IDX_DIM, ki, 0.0)
    kia_ref[0] = kz.astype(lowp)
    kib_ref[0] = pltpu.roll(kz, IDX_DIM, 1).astype(lowp)
    wo_ref[0] = sm[:, IDX_DIM:IDX_DIM + IDX_HEADS] * IDX_SCALE


def _attn_prep(p3d, pos, qn, kn):
    b, t_len, _ = p3d.shape
    tt = min(t_len, 256)
    lowp = _branch_dtype(tt)
    c128, s128, c64, s64 = _rope_tables(pos)
    tab = pl.BlockSpec((tt, LANES), lambda i, t: (t, 0))
    hv = pl.BlockSpec((1, LANES), lambda i, t: (0, 0))
    o_row = pl.BlockSpec((1, tt, BR_W), lambda i, t: (i, t, 0))
    o_l = pl.BlockSpec((1, tt, LANES), lambda i, t: (i, t, 0))
    big_f = jax.ShapeDtypeStruct((b, t_len, BR_W), F32)
    big_b = jax.ShapeDtypeStruct((b, t_len, BR_W), lowp)
    outs = pl.pallas_call(
        _attn_prep_kernel,
        grid=(b, t_len // tt),
        in_specs=[_row_spec(tt, 'attn_q'), _row_spec(tt, 'attn_k'), _row_spec(tt, 'attn_v'), _row_spec(tt, 'idx_q'),
                  pl.BlockSpec((1, tt, SMALL_W), lambda i, t: (i, t, SMALL_BLK)),
                  hv, hv, tab, tab, tab, tab],
        out_specs=[o_row, o_row, o_row, o_row, o_row, o_row,
                   pl.BlockSpec((1, tt, IDX_DIM), lambda i, t: (i, t, 0)), o_l, o_l,
                   pl.BlockSpec((1, tt, IDX_HEADS), lambda i, t: (i, t, 0))],
        out_shape=[big_b, big_f, big_b, big_f, big_b, big_b,
                   jax.ShapeDtypeStruct((b, t_len, IDX_DIM), F32),
                   jax.ShapeDtypeStruct((b, t_len, LANES), lowp),
                   jax.ShapeDtypeStruct((b, t_len, LANES), lowp),
                   jax.ShapeDtypeStruct((b, t_len, IDX_HEADS), F32)],
        compiler_params=_params(("parallel", "parallel")),
        name="attn_prep",
    )(p3d, p3d, p3d, p3d, p3d, qn.reshape(1, LANES), kn.reshape(1, LANES), c128, s128, c64, s64)
    return tuple(o if o.dtype == F32 else o.astype(BF16) for o in outs)


def _sort_key(x):
    bits = pltpu.bitcast(x, I32)
    return jnp.where(bits < 0, bits ^ jnp.int32(0x7FFFFFFF), bits)


def _kth_largest_key(load_tile, n_tiles, rows, k_top):
    def count_tile(j, cand):
        x = load_tile(j)
        reps = x.shape[1] // LANES
        hit = jnp.where(x >= jnp.concatenate([cand] * reps, axis=1), 1.0, 0.0)
        part = hit[:, 0:LANES]
        for r in range(1, reps):
            part = part + hit[:, r * LANES:(r + 1) * LANES]
        return part

    def bit_step(i, lo):
        cand = lo + lax.shift_left(jnp.int32(1), jnp.int32(31) - i)
        if isinstance(n_tiles, int):
            acc = count_tile(0, cand)
            for j in range(1, n_tiles):
                acc = acc + count_tile(j, cand)
        else:
            acc = lax.fori_loop(0, n_tiles, lambda j, a: a + count_tile(j, cand),
                                jnp.zeros((rows, LANES), F32))
        cnt = jnp.broadcast_to(jnp.sum(acc, axis=-1, keepdims=True), (rows, LANES))
        return jnp.where(cnt >= k_top, cand, lo)

    lo = lax.fori_loop(0, 32, bit_step, jnp.full((rows, LANES), INT_MIN, I32))
    return jnp.maximum(lo, jnp.int32(INT_MIN + 1))


def _attn_prompt_kernel(q_ref, iq_ref, w_ref, gate_ref, k_ref, v_ref, kia_ref, kib_ref, o_ref,
                        key_scr, m_scr, l_scr, acc_scr, *, k_top, kc):
    qb = pl.program_id(1)
    n_chunks = ((qb + 1) * Q_BLOCK + kc - 1) // kc
    qpos = qb * Q_BLOCK + lax.broadcasted_iota(I32, (Q_BLOCK, kc), 0)
    w = w_ref[0]

    def score_chunk(c, carry):
        k0 = pl.multiple_of(c * kc, kc)
        ka = kia_ref[0, pl.ds(k0, kc), :]
        kb = kib_ref[0, pl.ds(k0, kc), :]
        sc = jnp.zeros((Q_BLOCK, kc), F32)
        for j in range(IDX_HEADS // 2):
            qp = iq_ref[0, :, j * LANES:(j + 1) * LANES]
            for hh, kk in ((0, ka), (1, kb)):
                rel = lax.dot_general(qp, kk, (((1,), (1,)), ((), ())), preferred_element_type=F32)
                h = 2 * j + hh
                sc = sc + jnp.maximum(rel, 0.0) * w[:, h:h + 1]
        spos = k0 + lax.broadcasted_iota(I32, (Q_BLOCK, kc), 1)
        key_scr[:, pl.ds(k0, kc)] = jnp.where(spos <= qpos, _sort_key(sc), jnp.int32(INT_MIN))
        return carry

    lax.fori_loop(0, n_chunks, score_chunk, 0)

    def load_tile(j):
        return key_scr[:, pl.ds(pl.multiple_of(j * kc, kc), kc)]

    reps = kc // LANES
    thr = _kth_largest_key(load_tile, n_chunks, Q_BLOCK, k_top)
    thr_c = jnp.concatenate([thr] * reps, axis=1)

    m_scr[...] = jnp.full(m_scr.shape, NEG_BIG, F32)
    l_scr[...] = jnp.zeros(l_scr.shape, F32)
    acc_scr[...] = jnp.zeros(acc_scr.shape, F32)

    def attend_chunk(c, carry):
        k0 = pl.multiple_of(c * kc, kc)
        sel = key_scr[:, pl.ds(k0, kc)] >= thr_c
        for h in range(ATTN_HEADS):
            sl = slice(h * LANES, (h + 1) * LANES)
            kh = k_ref[0, pl.ds(k0, kc), sl]
            vh = v_ref[0, pl.ds(k0, kc), sl]
            s = lax.dot_general(q_ref[0, :, sl], kh, (((1,), (1,)), ((), ())), preferred_element_type=F32)
            s = jnp.where(sel, s, NEG_BIG)
            m = m_scr[:, sl]
            m_new = jnp.maximum(m, jnp.broadcast_to(jnp.max(s, axis=-1, keepdims=True), (Q_BLOCK, LANES)))
            alpha = jnp.exp(m - m_new)
            p = jnp.exp(s - jnp.concatenate([m_new] * reps, axis=1))
            l_scr[:, sl] = l_scr[:, sl] * alpha + jnp.broadcast_to(jnp.sum(p, axis=-1, keepdims=True),
                                                                    (Q_BLOCK, LANES))
            acc_scr[:, sl] = acc_scr[:, sl] * alpha + jnp.dot(p.astype(BF16), vh, preferred_element_type=F32)
            m_scr[:, sl] = m_new
        return carry

    lax.fori_loop(0, n_chunks, attend_chunk, 0)
    o_ref[0] = (acc_scr[...] / l_scr[...] * _silu(gate_ref[0])).astype(BF16)


def _attn_prompt(prep, p3d):
    qb16, _, kb16, _, vb16, iqb16, _, kia, kib, w = prep
    b, t_len, _ = qb16.shape
    k_top = min(TOPK_MAX, t_len // 4)
    kc = min(256, t_len)
    assert t_len % kc == 0 and kc % Q_BLOCK == 0
    qrow = pl.BlockSpec((1, Q_BLOCK, BR_W), lambda i, t: (i, t, 0))

    def full(width):
        return pl.BlockSpec((1, t_len, width), lambda i, t: (i, 0, 0))

    return pl.pallas_call(
        functools.partial(_attn_prompt_kernel, k_top=k_top, kc=kc),
        grid=(b, t_len // Q_BLOCK),
        in_specs=[qrow, qrow,
                  pl.BlockSpec((1, Q_BLOCK, IDX_HEADS), lambda i, t: (i, t, 0)),
                  _row_spec(Q_BLOCK, 'attn_gate'),
                  full(BR_W), full(BR_W), full(LANES), full(LANES)],
        out_specs=pl.BlockSpec((1, Q_BLOCK, BR_W), lambda i, t: (i, t, 0)),
        out_shape=jax.ShapeDtypeStruct((b, t_len, BR_W), BF16),
        scratch_shapes=[pltpu.VMEM((Q_BLOCK, t_len), I32)] + [pltpu.VMEM((Q_BLOCK, BR_W), F32)] * 3,
        compiler_params=_params(("parallel", "arbitrary")),
        name="attn_prompt",
    )(qb16, iqb16, w, p3d, kb16, vb16, kia, kib)


SCORE_PAGES = 8
ATTEND_PAGES = 8


def _score_tile(qs, w, ki, t_new, new_rows):
    rel = lax.dot_general(qs, ki, (((1,), (1,)), ((), ())), preferred_element_type=F32)
    wr = jnp.maximum(rel, 0.0) * w
    sc = jnp.zeros((SUBLANES, PAGE_SIZE), F32)
    for t in range(t_new):
        sc = _put_row(sc, jnp.sum(wr[t * IDX_HEADS:(t + 1) * IDX_HEADS], axis=0, keepdims=True), t)
    qi = lax.broadcasted_iota(I32, sc.shape, 0)
    si = lax.broadcasted_iota(I32, sc.shape, 1)
    visible = qi < t_new
    if new_rows:
        visible = jnp.logical_and(visible, si <= qi)
    return jnp.where(visible, _sort_key(sc), jnp.int32(INT_MIN))


def _sample_scores_kernel(pt_ref, qs_ref, w_ref, *rest, t_new):
    pages = rest[:SCORE_PAGES]
    new_ref, o_ref, onew_ref = rest[SCORE_PAGES:]
    qs = qs_ref[0]
    w = w_ref[0]
    for j, page_ref in enumerate(pages):
        o_ref[0, j] = _score_tile(qs, w, page_ref[0, 0].astype(BF16), t_new, False)

    @pl.when(pl.program_id(1) == pl.num_programs(1) - 1)
    def _():
        onew_ref[0] = _score_tile(qs, w, new_ref[0], t_new, True)


def _sample_thresh_kernel(key_ref, knew_ref, o_ref, *, n_pages, k_top):
    def load_tile(j):
        return knew_ref[0] if j == n_pages else key_ref[0, j]

    o_ref[0] = _kth_largest_key(load_tile, n_pages + 1, SUBLANES, k_top)


def _sample_attend_kernel(pt_ref, q_ref, key_ref, knew_ref, thr_ref, gate_ref, exp_ref, *rest, t_new):
    kps = rest[:ATTEND_PAGES]
    vps = rest[ATTEND_PAGES:2 * ATTEND_PAGES]
    kn_ref, vn_ref, o_ref, m_scr, l_scr, acc_scr = rest[2 * ATTEND_PAGES:]
    p = pl.program_id(1)
    rows = t_new * ATTN_HEADS
    cols = PAGE_SIZE * ATTN_HEADS

    @pl.when(p == 0)
    def _():
        m_scr[...] = jnp.full(m_scr.shape, NEG_BIG, F32)
        l_scr[...] = jnp.zeros(l_scr.shape, F32)
        acc_scr[...] = jnp.zeros(acc_scr.shape, F32)

    q = q_ref[0]
    thr = thr_ref[0]

    def update(keys_list, kp_list, vp_list):
        n = len(keys_list)
        kp = jnp.concatenate(kp_list, axis=0) if n > 1 else kp_list[0]
        vp = jnp.concatenate(vp_list, axis=0) if n > 1 else vp_list[0]
        s = lax.dot_general(q, kp, (((1,), (1,)), ((), ())), preferred_element_type=F32)
        sels = []
        for keys in keys_list:
            sel = jnp.where(keys >= thr, 1.0, 0.0)
            sel = jnp.concatenate([jnp.broadcast_to(sel[t:t + 1, :], (ATTN_HEADS, PAGE_SIZE))
                                   for t in range(t_new)], axis=0)
            sels.append(jnp.dot(sel.astype(BF16), exp_ref[...], preferred_element_type=F32))
        sel = jnp.concatenate(sels, axis=1) if n > 1 else sels[0]
        own_head = (lax.broadcasted_iota(I32, s.shape, 0) % ATTN_HEADS
                    == lax.broadcasted_iota(I32, s.shape, 1) % ATTN_HEADS)
        s = jnp.where(jnp.logical_and(sel > 0.5, own_head), s, NEG_BIG)
        m = m_scr[...]
        m_new = jnp.maximum(m, jnp.max(s, axis=-1, keepdims=True))
        alpha = jnp.exp(m - m_new)
        pr = jnp.exp(s - m_new)
        l_scr[...] = l_scr[...] * alpha + jnp.sum(pr, axis=-1, keepdims=True)
        acc_scr[...] = acc_scr[...] * alpha + jnp.dot(pr.astype(BF16), vp, preferred_element_type=F32)
        m_scr[...] = m_new

    update([key_ref[0, j] for j in range(ATTEND_PAGES)],
           [kps[j][0, 0].astype(BF16) for j in range(ATTEND_PAGES)],
           [vps[j][0, 0].astype(BF16) for j in range(ATTEND_PAGES)])

    @pl.when(p == pl.num_programs(1) - 1)
    def _():
        update([knew_ref[0]], [kn_ref[0]], [vn_ref[0]])
        o_ref[0] = acc_scr[...] / l_scr[...] * _silu(gate_ref[0])


def _attn_sample(prep, p3d, cache_k, cache_v, cache_kidx, page_table, layer):
    qb16, _, kb16, _, vb16, iqb16, kif, _, _, w = prep
    b, t_new, _ = qb16.shape
    assert t_new <= SUBLANES
    n_pages = page_table.shape[1]
    assert n_pages % SCORE_PAGES == 0 and n_pages % ATTEND_PAGES == 0
    past = n_pages * PAGE_SIZE
    k_top = min(TOPK_MAX, (past + t_new) // 4)
    n_pool = cache_k.shape[0]
    cols = PAGE_SIZE * ATTN_HEADS
    rows = t_new * ATTN_HEADS
    ck = cache_k.reshape(n_pool, DEPTH, cols, ATTN_HEAD_DIM)
    cv = cache_v.reshape(n_pool, DEPTH, cols, ATTN_HEAD_DIM)
    n_qrow = t_new * IDX_HEADS
    qs = iqb16.reshape(b, n_qrow, IDX_DIM)
    w128 = jnp.broadcast_to(w.reshape(b, n_qrow, 1), (b, n_qrow, PAGE_SIZE))
    ki_new = jnp.pad(kif, ((0, 0), (0, PAGE_SIZE - t_new), (0, 0))).astype(BF16)

    def page(j, per_step):
        return lambda i, p, pt: (pt[i, p * per_step + j], layer, 0, 0)

    keys, keys_new = pl.pallas_call(
        functools.partial(_sample_scores_kernel, t_new=t_new),
        grid_spec=pltpu.PrefetchScalarGridSpec(
            num_scalar_prefetch=1,
            grid=(b, n_pages // SCORE_PAGES),
            in_specs=[pl.BlockSpec((1, n_qrow, IDX_DIM), lambda i, p, pt: (i, 0, 0)),
                      pl.BlockSpec((1, n_qrow, PAGE_SIZE), lambda i, p, pt: (i, 0, 0))]
                     + [pl.BlockSpec((1, 1, PAGE_SIZE, IDX_DIM), page(j, SCORE_PAGES)) for j in range(SCORE_PAGES)]
                     + [pl.BlockSpec((1, PAGE_SIZE, IDX_DIM), lambda i, p, pt: (i, 0, 0))],
            out_specs=[pl.BlockSpec((1, SCORE_PAGES, SUBLANES, PAGE_SIZE), lambda i, p, pt: (i, p, 0, 0)),
                       pl.BlockSpec((1, SUBLANES, PAGE_SIZE), lambda i, p, pt: (i, 0, 0))]),
        out_shape=[jax.ShapeDtypeStruct((b, n_pages, SUBLANES, PAGE_SIZE), I32),
                   jax.ShapeDtypeStruct((b, SUBLANES, PAGE_SIZE), I32)],
        compiler_params=_params(("parallel", "arbitrary")),
        name="sample_scores",
    )(page_table, qs, w128, *([cache_kidx] * SCORE_PAGES), ki_new)

    thr = pl.pallas_call(
        functools.partial(_sample_thresh_kernel, n_pages=n_pages, k_top=k_top),
        grid=(b,),
        in_specs=[pl.BlockSpec((1, n_pages, SUBLANES, PAGE_SIZE), lambda i: (i, 0, 0, 0)),
                  pl.BlockSpec((1, SUBLANES, PAGE_SIZE), lambda i: (i, 0, 0))],
        out_specs=pl.BlockSpec((1, SUBLANES, LANES), lambda i: (i, 0, 0)),
        out_shape=jax.ShapeDtypeStruct((b, SUBLANES, LANES), I32),
        compiler_params=_params(("parallel",)),
        name="sample_thresh",
    )(keys, keys_new)

    expand = jnp.asarray(np.arange(PAGE_SIZE)[:, None] == np.arange(cols)[None, :] // ATTN_HEADS, dtype=BF16)
    pad_new = lambda x: jnp.pad(x.reshape(b, rows, ATTN_HEAD_DIM), ((0, 0), (0, cols - rows), (0, 0)))
    gate = _seg(p3d, 'attn_gate').reshape(b, rows, ATTN_HEAD_DIM)
    whole = lambda r, c: pl.BlockSpec((1, r, c), lambda i, p, pt: (i, 0, 0))
    kv_page = lambda j: pl.BlockSpec((1, 1, cols, ATTN_HEAD_DIM), page(j, ATTEND_PAGES))
    out = pl.pallas_call(
        functools.partial(_sample_attend_kernel, t_new=t_new),
        grid_spec=pltpu.PrefetchScalarGridSpec(
            num_scalar_prefetch=1,
            grid=(b, n_pages // ATTEND_PAGES),
            in_specs=[whole(rows, ATTN_HEAD_DIM),
                      pl.BlockSpec((1, ATTEND_PAGES, SUBLANES, PAGE_SIZE), lambda i, p, pt: (i, p, 0, 0)),
                      whole(SUBLANES, PAGE_SIZE), whole(SUBLANES, LANES), whole(rows, ATTN_HEAD_DIM),
                      pl.BlockSpec((PAGE_SIZE, cols), lambda i, p, pt: (0, 0))]
                     + [kv_page(j) for j in range(ATTEND_PAGES)] + [kv_page(j) for j in range(ATTEND_PAGES)]
                     + [whole(cols, ATTN_HEAD_DIM), whole(cols, ATTN_HEAD_DIM)],
            out_specs=whole(rows, ATTN_HEAD_DIM),
            scratch_shapes=[pltpu.VMEM((rows, 1), F32), pltpu.VMEM((rows, 1), F32),
                            pltpu.VMEM((rows, ATTN_HEAD_DIM), F32)]),
        out_shape=jax.ShapeDtypeStruct((b, rows, ATTN_HEAD_DIM), F32),
        compiler_params=_params(("parallel", "arbitrary")),
        name="sample_attend",
    )(page_table, qb16.reshape(b, rows, ATTN_HEAD_DIM), keys, keys_new, thr, gate, expand,
      *([ck] * ATTEND_PAGES), *([cv] * ATTEND_PAGES), pad_new(kb16), pad_new(vb16))
    return out.reshape(b, t_new, BR_W)


def _layer(x, lw, pos0, pool_prev, shift_prev, wkv_prev, paged):
    b, t_len, _ = x.shape
    x2d = x.reshape(b * t_len, D_MODEL)
    pos = pos0 + jnp.arange(t_len, dtype=I32)
    p2d = _inproj(x2d, lw['norm_g'], lw['w_in'])
    p3d = p2d.reshape(b, t_len, IN_COLS_PAD)

    pool_br, pool_state = _pool(p3d, pool_prev, lw['pool_w'], lw['pool_scale'], pos0)
    gmlp_br, gmlp_vn = _gmlp(p3d, lw['gmlp_ln_g'], lw['gmlp_ln_b'], lw['gmlp_wm'], lw['gmlp_bsb'],
                             emit_vn=paged is not None)
    seqs, shift_state = _rwkv_prep(p3d, shift_prev, lw)
    y, wkv_state = _rwkv_scan(seqs[:6], wkv_prev)
    rwkv_br = _rwkv_post(y, seqs[6], p3d, lw['rwkv_lnx_g'], lw['rwkv_lnx_b'])
    prep = _attn_prep(p3d, pos, lw['attn_qn'], lw['attn_kn'])
    if paged is None:
        attn_br = _attn_prompt(prep, p3d)
    else:
        attn_br = _attn_sample(prep, p3d, *paged)

    flat = lambda t: t.reshape(b * t_len, BR_W).astype(BF16)
    merged = _merge([flat(pool_br), flat(gmlp_br), flat(rwkv_br), flat(attn_br)], lw['w_branch'], p2d)
    y_out = _outproj(x2d, merged, lw['w_out']).reshape(b, t_len, D_MODEL)
    hd = lambda t: t.reshape(b, t_len, ATTN_HEADS, ATTN_HEAD_DIM)
    return y_out, (hd(prep[1]), hd(prep[3]), prep[6], pool_state, shift_state, wkv_state, gmlp_vn)


def _lora_weights(w2, a2):
    z = jnp.zeros_like(w2)
    return jnp.concatenate([w2, z], axis=0).astype(BF16), jnp.concatenate([z, a2], axis=0).astype(BF16)


def kernel(x_prompt, x_sample, cache_k, cache_v, cache_kidx, page_table, state_pool, state_shift, state_wkv,
           norm_g, w_in, pool_w, pool_scale, gmlp_ln_g, gmlp_ln_b, gmlp_ws, gmlp_bs, rwkv_mu, rwkv_w0, rwkv_w2,
           rwkv_a0, rwkv_a2, rwkv_kk, rwkv_ka, rwkv_rk, rwkv_lnx_g, rwkv_lnx_b, attn_qn, attn_kn, w_branch, w_out):
    past = page_table.shape[1] * PAGE_SIZE
    bp = x_prompt.shape[0]
    w_in_p = _permute_w_in(w_in)
    w_branch_b = w_branch.astype(BF16)
    w_out_b = w_out.astype(BF16)
    pool_w_b = pool_w.astype(BF16)
    causal = jnp.tril(jnp.ones((GMLP_CHUNK, GMLP_CHUNK), dtype=bool))
    gmlp_wm = jnp.where(causal[None, None], gmlp_ws, 0.0).astype(BF16)
    gmlp_bsb = jnp.broadcast_to(gmlp_bs[..., None], gmlp_bs.shape + (GMLP_GROUP_W,))
    xp, xs = x_prompt, x_sample
    outs_p, outs_s = [], []
    for l in range(DEPTH):
        wl, al = _lora_weights(rwkv_w2[l], rwkv_a2[l])
        lw = {
            'norm_g': norm_g[l], 'w_in': w_in_p[l], 'pool_w': pool_w_b[l], 'pool_scale': pool_scale[l],
            'gmlp_ln_g': gmlp_ln_g[l], 'gmlp_ln_b': gmlp_ln_b[l], 'gmlp_wm': gmlp_wm[l], 'gmlp_bsb': gmlp_bsb[l],
            'rwkv_mu': rwkv_mu[l], 'rwkv_w0': rwkv_w0[l], 'rwkv_wl': wl, 'rwkv_a0': rwkv_a0[l],
            'rwkv_al': al, 'rwkv_kk': rwkv_kk[l], 'rwkv_ka': rwkv_ka[l], 'rwkv_rk': rwkv_rk[l].reshape(BR_W),
            'rwkv_lnx_g': rwkv_lnx_g[l], 'rwkv_lnx_b': rwkv_lnx_b[l], 'attn_qn': attn_qn[l],
            'attn_kn': attn_kn[l], 'w_branch': w_branch_b[l], 'w_out': w_out_b[l],
        }
        xp, st_p = _layer(
            xp, lw, 0,
            jnp.zeros((bp, POOL_STATE, BR_W), F32),
            jnp.zeros((bp, RWKV_SHIFT_W), F32),
            jnp.zeros((bp, RWKV_HEADS, RWKV_HEAD_DIM, RWKV_HEAD_DIM), F32),
            None)
        xs, st_s = _layer(xs, lw, past, state_pool[:, l], state_shift[:, l], state_wkv[:, l],
                          (cache_k, cache_v, cache_kidx, page_table, l))
        outs_p.append(st_p)
        outs_s.append(st_s)
    stk = lambda outs, i: jnp.stack([o[i] for o in outs], axis=1)
    k_p, v_p, ki_p = stk(outs_p, 0), stk(outs_p, 1), stk(outs_p, 2)
    pool_p, shift_p, wkv_p = stk(outs_p, 3), stk(outs_p, 4), stk(outs_p, 5)
    k_s, v_s, ki_s = stk(outs_s, 0), stk(outs_s, 1), stk(outs_s, 2)
    pool_s, shift_s, wkv_s = stk(outs_s, 3), stk(outs_s, 4), stk(outs_s, 5)
    gmlp_v_s = stk(outs_s, 6)
    return (xp, xs, k_p, v_p, ki_p, k_s, v_s, ki_s, pool_p, pool_s, shift_p, shift_s, wkv_p, wkv_s, gmlp_v_s)
```

```python
import functools

import numpy as np
import jax
import jax.numpy as jnp
from jax import lax
from jax.experimental import pallas as pl
from jax.experimental.pallas import tpu as pltpu

F32 = jnp.float32
BF16 = jnp.bfloat16
I32 = jnp.int32

D_MODEL = 2048
DEPTH = 4
PAGE_SIZE = 128
BR_W = D_MODEL // 2
N_BRANCH = 4
POOL_WINDOWS = (2, 4, 8, 16)
POOL_GROUPS = 4
POOL_GROUP_W = BR_W // POOL_GROUPS
POOL_STATE = 15
POOL_HALO = 16
GMLP_CHUNK = 128
GMLP_GROUPS = 8
GMLP_GROUP_W = BR_W // GMLP_GROUPS
GMLP_LN_EPS = 1e-5
RWKV_HEAD_DIM = 64
RWKV_HEADS = BR_W // RWKV_HEAD_DIM
RWKV_PAIRS = RWKV_HEADS // 2
RWKV_W_LORA = 64
RWKV_A_LORA = 64
RWKV_SHIFT_W = 3 * BR_W + RWKV_W_LORA + RWKV_A_LORA
RWKV_LN_EPS = 64e-5
ATTN_HEAD_DIM = 128
ATTN_HEADS = BR_W // ATTN_HEAD_DIM
IDX_HEADS = 16
IDX_DIM = 64
IDX_SCALE = (IDX_HEADS * IDX_DIM) ** -0.5
TOPK_MAX = 256
Q_BLOCK = 128
ROPE_THETA = 10000.0
NORM_EPS = 1e-6
LANES = 128
SUBLANES = 8
BF16_ROWS = 16
INT_MIN = -2 ** 31
NEG_BIG = -1e30

SEG_SRC = (
    ('pool_u', BR_W), ('pool_gate', BR_W),
    ('gmlp_u', BR_W), ('gmlp_v', BR_W), ('gmlp_gate', BR_W),
    ('rwkv_shift', RWKV_SHIFT_W), ('rwkv_gate', BR_W),
    ('attn_q', BR_W), ('attn_k', BR_W), ('attn_v', BR_W),
    ('idx_q', IDX_HEADS * IDX_DIM), ('idx_k', IDX_DIM), ('idx_w', IDX_HEADS),
    ('attn_gate', BR_W), ('merge', N_BRANCH * D_MODEL),
)
SEG_DST_ORDER = ('pool_u', 'pool_gate', 'gmlp_u', 'gmlp_v', 'gmlp_gate', 'rwkv_gate',
                 'attn_q', 'attn_k', 'attn_v', 'idx_q', 'attn_gate', 'merge',
                 'rwkv_shift', 'idx_k', 'idx_w')
IN_COLS = sum(w for _, w in SEG_SRC)
IN_COLS_PAD = 23040
VMEM_LIMIT = 56 * 1024 * 1024


def _seg_offsets():
    src, o = {}, 0
    for name, w in SEG_SRC:
        src[name] = (o, w)
        o += w
    dst, o = {}, 0
    for name in SEG_DST_ORDER:
        dst[name] = o
        o += src[name][1]
    return src, dst


SRC_OFF, DST_OFF = _seg_offsets()
SMALL_W = 256
SMALL_BLK = (DST_OFF['rwkv_shift'] + 3 * BR_W) // SMALL_W
assert (DST_OFF['rwkv_shift'] + 3 * BR_W) % SMALL_W == 0
assert DST_OFF['idx_k'] == SMALL_BLK * SMALL_W + 128 and DST_OFF['idx_w'] == SMALL_BLK * SMALL_W + 192
assert RWKV_W_LORA + RWKV_A_LORA == LANES


def _cblk(name):
    assert DST_OFF[name] % BR_W == 0
    return DST_OFF[name] // BR_W


def _permute_w_in(w_in):
    parts = [w_in[..., SRC_OFF[n][0]:SRC_OFF[n][0] + SRC_OFF[n][1]] for n in SEG_DST_ORDER]
    parts.append(jnp.zeros(w_in.shape[:-1] + (IN_COLS_PAD - IN_COLS,), w_in.dtype))
    return jnp.concatenate(parts, axis=-1).astype(BF16)


def _seg(p, name):
    return p[..., DST_OFF[name]:DST_OFF[name] + SRC_OFF[name][1]]


def _params(sem):
    return pltpu.CompilerParams(dimension_semantics=sem, vmem_limit_bytes=VMEM_LIMIT)


def _branch_dtype(tt):
    return BF16 if tt % BF16_ROWS == 0 else F32


def _row_spec(tt, name):
    c = _cblk(name)
    return pl.BlockSpec((1, tt, BR_W), lambda i, t: (i, t, c))


def _silu(x):
    return x * jax.nn.sigmoid(x)


def _head_ones():
    r = np.arange(2 * LANES)[:, None] % LANES
    c = np.arange(LANES)[None, :]
    return jnp.asarray((r // RWKV_HEAD_DIM) == (c // RWKV_HEAD_DIM), dtype=BF16)


def _seg_sum(x, ones2):
    hi = x.astype(BF16)
    lo = (x - hi.astype(F32)).astype(BF16)
    return jnp.dot(jnp.concatenate([hi, lo], axis=1), ones2, preferred_element_type=F32)


def _put_row(dst, row, i):
    sub = lax.broadcasted_iota(I32, dst.shape, 0)
    return jnp.where(sub == i, jnp.broadcast_to(row, dst.shape), dst)


def _inproj_kernel(x_ref, g_ref, w_ref, o_ref, h_scr):
    @pl.when(pl.program_id(1) == 0)
    def _():
        x = x_ref[...]
        ms = jnp.mean(x * x, axis=-1, keepdims=True)
        h_scr[...] = (x * lax.rsqrt(ms + NORM_EPS) * g_ref[...]).astype(BF16)

    o_ref[...] = jnp.dot(h_scr[...], w_ref[...], preferred_element_type=F32)


def _inproj(x2d, g, w_bf16):
    m = x2d.shape[0]
    tm = min(m, 1024)
    tn = 1536
    return pl.pallas_call(
        _inproj_kernel,
        grid=(m // tm, IN_COLS_PAD // tn),
        in_specs=[pl.BlockSpec((tm, D_MODEL), lambda i, j: (i, 0)),
                  pl.BlockSpec((1, D_MODEL), lambda i, j: (0, 0)),
                  pl.BlockSpec((D_MODEL, tn), lambda i, j: (0, j))],
        out_specs=pl.BlockSpec((tm, tn), lambda i, j: (i, j)),
        out_shape=jax.ShapeDtypeStruct((m, IN_COLS_PAD), F32),
        scratch_shapes=[pltpu.VMEM((tm, D_MODEL), BF16)],
        compiler_params=_params(("parallel", "arbitrary")),
        name="inproj",
    )(x2d, g.reshape(1, D_MODEL), w_bf16)


def _merge_kernel(b0, b1, b2, b3, wb_ref, g0, g1, g2, g3, o_ref):
    acc = None
    for n, (b_ref, g_ref) in enumerate(((b0, g0), (b1, g1), (b2, g2), (b3, g3))):
        proj = jnp.dot(b_ref[...], wb_ref[n], preferred_element_type=F32)
        term = jax.nn.sigmoid(g_ref[...]) * proj
        acc = term if acc is None else acc + term
    o_ref[...] = acc.astype(BF16)


def _merge(branches, wb_bf16, p2d):
    m = p2d.shape[0]
    tm = min(m, 512)
    tn = 512
    goff = DST_OFF['merge'] // tn
    gstep = D_MODEL // tn
    bspec = pl.BlockSpec((tm, BR_W), lambda i, j: (i, 0))

    def gate_map(n):
        return lambda i, j: (i, goff + n * gstep + j)

    gspecs = [pl.BlockSpec((tm, tn), gate_map(n)) for n in range(N_BRANCH)]
    return pl.pallas_call(
        _merge_kernel,
        grid=(m // tm, D_MODEL // tn),
        in_specs=[bspec, bspec, bspec, bspec,
                  pl.BlockSpec((N_BRANCH, BR_W, tn), lambda i, j: (0, 0, j))] + gspecs,
        out_specs=pl.BlockSpec((tm, tn), lambda i, j: (i, j)),
        out_shape=jax.ShapeDtypeStruct((m, D_MODEL), BF16),
        compiler_params=_params(("parallel", "arbitrary")),
        name="merge",
    )(*branches, wb_bf16, p2d, p2d, p2d, p2d)


def _outproj_kernel(x_ref, m_ref, w_ref, o_ref):
    o_ref[...] = x_ref[...] + jnp.dot(m_ref[...], w_ref[...], preferred_element_type=F32)


def _outproj(x2d, merged, wo_bf16):
    m = x2d.shape[0]
    tm = min(m, 1024)
    tn = 512
    return pl.pallas_call(
        _outproj_kernel,
        grid=(m // tm, D_MODEL // tn),
        in_specs=[pl.BlockSpec((tm, tn), lambda i, j: (i, j)),
                  pl.BlockSpec((tm, D_MODEL), lambda i, j: (i, 0)),
                  pl.BlockSpec((D_MODEL, tn), lambda i, j: (0, j))],
        out_specs=pl.BlockSpec((tm, tn), lambda i, j: (i, j)),
        out_shape=jax.ShapeDtypeStruct((m, D_MODEL), F32),
        compiler_params=_params(("parallel", "arbitrary")),
        name="outproj",
    )(x2d, merged, wo_bf16)


def _pool_kernel(u_ref, gate_ref, prev_ref, w_ref, scale_ref, o_ref, st_ref, ext, *, tt, pos0):
    t = pl.program_id(1)
    nt = pl.num_programs(1)

    @pl.when(t == 0)
    def _():
        ext[0:1, :] = jnp.zeros((1, BR_W), F32)
        ext[1:POOL_HALO, :] = prev_ref[0]

    @pl.when(t > 0)
    def _():
        ext[0:POOL_HALO, :] = ext[tt:tt + POOL_HALO, :]

    ext[POOL_HALO:POOL_HALO + tt, :] = u_ref[0]
    pos = pos0 + t * tt + lax.broadcasted_iota(I32, (tt, POOL_GROUP_W), 0)
    for g, w in enumerate(POOL_WINDOWS):
        sl = slice(g * POOL_GROUP_W, (g + 1) * POOL_GROUP_W)
        cur = ext[POOL_HALO:POOL_HALO + tt, sl]
        acc = cur
        for i in range(1, w):
            acc = acc + ext[POOL_HALO - i:POOL_HALO - i + tt, sl]
        cnt = jnp.minimum(w, pos + 1).astype(F32)
        pooled = acc / cnt - cur
        mixed = jnp.dot(pooled.astype(BF16), w_ref[g], preferred_element_type=F32)
        o_ref[0, :, sl] = (mixed * scale_ref[:, sl] * _silu(gate_ref[0, :, sl])).astype(o_ref.dtype)

    @pl.when(t == nt - 1)
    def _():
        st_ref[0] = ext[tt + 1:tt + POOL_HALO, :]


def _pool(p3d, prev, pool_w_bf16, scale, pos0):
    b, t_len, _ = p3d.shape
    tt = min(t_len, 512)
    return pl.pallas_call(
        functools.partial(_pool_kernel, tt=tt, pos0=pos0),
        grid=(b, t_len // tt),
        in_specs=[_row_spec(tt, 'pool_u'), _row_spec(tt, 'pool_gate'),
                  pl.BlockSpec((1, POOL_STATE, BR_W), lambda i, t: (i, 0, 0)),
                  pl.BlockSpec((POOL_GROUPS, POOL_GROUP_W, POOL_GROUP_W), lambda i, t: (0, 0, 0)),
                  pl.BlockSpec((1, BR_W), lambda i, t: (0, 0))],
        out_specs=[pl.BlockSpec((1, tt, BR_W), lambda i, t: (i, t, 0)),
                   pl.BlockSpec((1, POOL_STATE, BR_W), lambda i, t: (i, 0, 0))],
        out_shape=[jax.ShapeDtypeStruct((b, t_len, BR_W), _branch_dtype(tt)),
                   jax.ShapeDtypeStruct((b, POOL_STATE, BR_W), F32)],
        scratch_shapes=[pltpu.VMEM((POOL_HALO + tt, BR_W), F32)],
        compiler_params=_params(("parallel", "arbitrary")),
        name="pool",
    )(p3d, p3d, prev, pool_w_bf16, scale.reshape(1, BR_W))


def _gmlp_kernel(u_ref, v_ref, gate_ref, lng_ref, lnb_ref, wm_ref, bsb_ref, o_ref, *rest, tt, emit_vn):
    if emit_vn:
        vn_ref = rest[0]
        rest = rest[1:]
    v = v_ref[0]
    mu = jnp.mean(v, axis=-1, keepdims=True)
    var = jnp.mean(jnp.square(v - mu), axis=-1, keepdims=True)
    vn = (v - mu) * lax.rsqrt(var + GMLP_LN_EPS) * lng_ref[...] + lnb_ref[...]
    if emit_vn:
        vn_ref[0] = vn
    if tt % GMLP_CHUNK == 0:
        rows = GMLP_CHUNK
        n_chunks = tt // GMLP_CHUNK
        vnb = vn.astype(BF16)
    else:
        pad = rest[0]
        rows = tt
        n_chunks = 1
        pad[...] = jnp.zeros_like(pad)
        pad[0:tt, :] = vn
        vnb = pad[...].astype(BF16)
    for c in range(n_chunks):
        for g in range(GMLP_GROUPS):
            sl = slice(g * GMLP_GROUP_W, (g + 1) * GMLP_GROUP_W)
            rs = slice(c * rows, (c + 1) * rows)
            vc = vnb[c * GMLP_CHUNK:(c + 1) * GMLP_CHUNK, sl]
            mixed = jnp.dot(wm_ref[g], vc, preferred_element_type=F32) + bsb_ref[g]
            o_ref[0, rs, sl] = (u_ref[0, rs, sl] * mixed[0:rows] * _silu(gate_ref[0, rs, sl])).astype(o_ref.dtype)


def _gmlp(p3d, ln_g, ln_b, wm_bf16, bsb, emit_vn):
    b, t_len, _ = p3d.shape
    tt = min(t_len, 512)
    assert tt % GMLP_CHUNK == 0 or (tt == t_len and tt < GMLP_CHUNK)
    out_specs = [pl.BlockSpec((1, tt, BR_W), lambda i, t: (i, t, 0))]
    out_shape = [jax.ShapeDtypeStruct((b, t_len, BR_W), _branch_dtype(tt))]
    if emit_vn:
        out_specs.append(pl.BlockSpec((1, tt, BR_W), lambda i, t: (i, t, 0)))
        out_shape.append(jax.ShapeDtypeStruct((b, t_len, BR_W), F32))
    scratch = [] if tt % GMLP_CHUNK == 0 else [pltpu.VMEM((GMLP_CHUNK, BR_W), F32)]
    res = pl.pallas_call(
        functools.partial(_gmlp_kernel, tt=tt, emit_vn=emit_vn),
        grid=(b, t_len // tt),
        in_specs=[_row_spec(tt, 'gmlp_u'), _row_spec(tt, 'gmlp_v'), _row_spec(tt, 'gmlp_gate'),
                  pl.BlockSpec((1, BR_W), lambda i, t: (0, 0)),
                  pl.BlockSpec((1, BR_W), lambda i, t: (0, 0)),
                  pl.BlockSpec((GMLP_GROUPS, GMLP_CHUNK, GMLP_CHUNK), lambda i, t: (0, 0, 0)),
                  pl.BlockSpec((GMLP_GROUPS, GMLP_CHUNK, GMLP_GROUP_W), lambda i, t: (0, 0, 0))],
        out_specs=out_specs,
        out_shape=out_shape,
        scratch_shapes=scratch,
        compiler_params=_params(("parallel", "parallel")),
        name="gmlp",
    )(p3d, p3d, p3d, ln_g.reshape(1, BR_W), ln_b.reshape(1, BR_W), wm_bf16, bsb)
    return (res[0], res[1]) if emit_vn else (res[0], None)


def _rwkv_prep_kernel(r_ref, k_ref, v_ref, sm_ref, prev_ref, mu_ref, w0_ref, a0_ref, wl_ref, al_ref,
                      kk_ref, ka_ref, rk_ref, ones_ref,
                      ro_ref, do_ref, ko_ref, vo_ref, ao_ref, bo_ref, bonus_ref, st_ref,
                      buf, *, tt):
    t = pl.program_id(1)
    nt = pl.num_programs(1)

    @pl.when(t == 0)
    def _():
        buf[7:8, :] = prev_ref[0]

    @pl.when(t > 0)
    def _():
        buf[7:8, :] = buf[7 + tt:8 + tt, :]

    buf[8:8 + tt, 0:BR_W] = r_ref[0]
    buf[8:8 + tt, BR_W:2 * BR_W] = k_ref[0]
    buf[8:8 + tt, 2 * BR_W:3 * BR_W] = v_ref[0]
    buf[8:8 + tt, 3 * BR_W:RWKV_SHIFT_W] = sm_ref[0, :, 0:LANES]

    @pl.when(t == nt - 1)
    def _():
        st_ref[0] = buf[7 + tt:8 + tt, :]

    ones2 = ones_ref[...]

    def mixed(lo, hi):
        cur = buf[8:8 + tt, lo:hi]
        prv = buf[7:7 + tt, lo:hi]
        return cur + (prv - cur) * mu_ref[:, lo:hi]

    zs = mixed(3 * BR_W, RWKV_SHIFT_W)
    w_pre = w0_ref[...] + jnp.dot(jnp.tanh(zs).astype(BF16), wl_ref[...], preferred_element_type=F32)
    w = -jax.nn.softplus(-w_pre) - 0.5
    do_ref[0] = jnp.exp(-jnp.exp(w))
    a = jax.nn.sigmoid(a0_ref[...] + jnp.dot(zs.astype(BF16), al_ref[...], preferred_element_type=F32))
    r = mixed(0, BR_W)
    k = mixed(BR_W, 2 * BR_W)
    v = mixed(2 * BR_W, 3 * BR_W)
    ro_ref[0] = r
    vo_ref[0] = v
    k2 = k * (1.0 + (a - 1.0) * ka_ref[...])
    ko_ref[0] = k2
    kk = k * kk_ref[...]
    rkk = r * k2 * rk_ref[...]
    for j in range(RWKV_PAIRS):
        sl = slice(j * LANES, (j + 1) * LANES)
        kkj = kk[:, sl]
        nrm = jnp.sqrt(_seg_sum(kkj * kkj, ones2))
        kkn = kkj / jnp.maximum(nrm, 1e-12)
        ao_ref[0, :, sl] = -kkn
        bo_ref[0, :, sl] = kkn * a[:, sl]
        bonus_ref[0, :, sl] = _seg_sum(rkk[:, sl], ones2) * v[:, sl]


def _rwkv_prep(p3d, shift_prev, lw):
    b, t_len, _ = p3d.shape
    tt = min(t_len, 256)
    base = _cblk('rwkv_shift')

    def row(c):
        return pl.BlockSpec((1, tt, BR_W), lambda i, t: (i, t, c))

    vec = pl.BlockSpec((1, BR_W), lambda i, t: (0, 0))
    out_row = pl.BlockSpec((1, tt, BR_W), lambda i, t: (i, t, 0))
    big = jax.ShapeDtypeStruct((b, t_len, BR_W), F32)
    outs = pl.pallas_call(
        functools.partial(_rwkv_prep_kernel, tt=tt),
        grid=(b, t_len // tt),
        in_specs=[row(base), row(base + 1), row(base + 2),
                  pl.BlockSpec((1, tt, SMALL_W), lambda i, t: (i, t, SMALL_BLK)),
                  pl.BlockSpec((1, 1, RWKV_SHIFT_W), lambda i, t: (i, 0, 0)),
                  pl.BlockSpec((1, RWKV_SHIFT_W), lambda i, t: (0, 0)),
                  vec, vec,
                  pl.BlockSpec((LANES, BR_W), lambda i, t: (0, 0)),
                  pl.BlockSpec((LANES, BR_W), lambda i, t: (0, 0)),
                  vec, vec, vec,
                  pl.BlockSpec((2 * LANES, LANES), lambda i, t: (0, 0))],
        out_specs=[out_row] * 7 + [pl.BlockSpec((1, 1, RWKV_SHIFT_W), lambda i, t: (i, 0, 0))],
        out_shape=[big] * 7 + [jax.ShapeDtypeStruct((b, 1, RWKV_SHIFT_W), F32)],
        scratch_shapes=[pltpu.VMEM((8 + tt, RWKV_SHIFT_W), F32)],
        compiler_params=_params(("parallel", "arbitrary")),
        name="rwkv_prep",
    )(p3d, p3d, p3d, p3d, shift_prev.reshape(b, 1, RWKV_SHIFT_W), lw['rwkv_mu'].reshape(1, RWKV_SHIFT_W),
      lw['rwkv_w0'].reshape(1, BR_W), lw['rwkv_a0'].reshape(1, BR_W), lw['rwkv_wl'], lw['rwkv_al'],
      lw['rwkv_kk'].reshape(1, BR_W), lw['rwkv_ka'].reshape(1, BR_W), lw['rwkv_rk'].reshape(1, BR_W),
      _head_ones())
    return outs[:7], outs[7].reshape(b, RWKV_SHIFT_W)


def _rwkv_scan_kernel(r_ref, d_ref, k_ref, v_ref, a_ref, b_ref, s0_ref, ones_ref, y_ref, sf_ref,
                      s_scr, dm_scr, dmb_scr, *, nb, tc):
    t = pl.program_id(1)
    nt = pl.num_programs(1)
    rows = nb * RWKV_PAIRS * RWKV_HEAD_DIM
    grp = min(SUBLANES, tc)

    @pl.when(t == 0)
    def _():
        s_scr[...] = s0_ref[...].reshape(rows, LANES)
        ri = lax.broadcasted_iota(I32, (rows, LANES), 0) % RWKV_HEAD_DIM
        ci = lax.broadcasted_iota(I32, (rows, LANES), 1) % RWKV_HEAD_DIM
        dm_scr[...] = jnp.where(ri == ci, 1.0, 0.0)
        dmb_scr[...] = jnp.where(ri == ci, 1.0, 0.0).astype(BF16)

    ones2 = ones_ref[...]
    first_head = lax.broadcasted_iota(I32, (rows, LANES), 1) < RWKV_HEAD_DIM

    def spread(tile, i, dtype=F32):
        pieces = []
        for bb in range(nb):
            for p in range(RWKV_PAIRS):
                row = tile[bb, i:i + 1, p * LANES:(p + 1) * LANES].astype(dtype)
                pieces.append(jnp.broadcast_to(row, (RWKV_HEAD_DIM, LANES)))
        return jnp.concatenate(pieces, axis=0)

    def group(gi, carry):
        g0 = pl.multiple_of(gi * grp, grp)
        rt = r_ref[:, pl.ds(g0, grp), :]
        dt = d_ref[:, pl.ds(g0, grp), :]
        kt = k_ref[:, pl.ds(g0, grp), :]
        vt = v_ref[:, pl.ds(g0, grp), :]
        at = a_ref[:, pl.ds(g0, grp), :]
        bt = b_ref[:, pl.ds(g0, grp), :]
        vt_hi = vt.astype(BF16).astype(F32)
        vt_lo = vt - vt_hi
        ys = [jnp.zeros((grp, LANES), F32) for _ in range(nb * RWKV_PAIRS)]
        for i in range(grp):
            dm = dm_scr[...]
            s = s_scr[...]
            sa = _seg_sum(s * spread(at, i), ones2)
            dmb = dmb_scr[...]
            vcol = jnp.dot(jnp.concatenate([dmb * spread(vt_hi, i, BF16), dmb * spread(vt_lo, i, BF16)], axis=1),
                           ones2, preferred_element_type=F32)
            s = s * spread(dt, i) + sa * spread(bt, i) + vcol * spread(kt, i)
            s_scr[...] = s
            q = s * spread(rt, i)
            y0 = jnp.sum(jnp.where(first_head, q, 0.0), axis=-1, keepdims=True)
            y1 = jnp.sum(jnp.where(first_head, 0.0, q), axis=-1, keepdims=True)
            yd = jnp.where(first_head, y0, y1) * dm
            for q in range(nb * RWKV_PAIRS):
                yrow = jnp.sum(yd[q * RWKV_HEAD_DIM:(q + 1) * RWKV_HEAD_DIM], axis=0, keepdims=True)
                ys[q] = _put_row(ys[q], yrow, i)
        for bb in range(nb):
            for p in range(RWKV_PAIRS):
                y_ref[bb, p, pl.ds(g0, grp), :] = ys[bb * RWKV_PAIRS + p]
        return carry

    lax.fori_loop(0, tc // grp, group, 0)

    @pl.when(t == nt - 1)
    def _():
        sf_ref[...] = s_scr[...].reshape(nb, RWKV_PAIRS, RWKV_HEAD_DIM, LANES)


def _pair_state(s):
    b = s.shape[0]
    s = s.reshape(b, RWKV_PAIRS, 2, RWKV_HEAD_DIM, RWKV_HEAD_DIM)
    return jnp.transpose(s, (0, 1, 3, 2, 4)).reshape(b, RWKV_PAIRS, RWKV_HEAD_DIM, LANES)


def _unpair_state(s):
    b = s.shape[0]
    s = s.reshape(b, RWKV_PAIRS, RWKV_HEAD_DIM, 2, RWKV_HEAD_DIM)
    return jnp.transpose(s, (0, 1, 3, 2, 4)).reshape(b, RWKV_HEADS, RWKV_HEAD_DIM, RWKV_HEAD_DIM)


def _rwkv_scan(seqs, wkv_prev):
    r, d, k, v, a, bv = seqs
    b, t_len, _ = r.shape
    nb = 2
    tc = min(t_len, 256)
    row = pl.BlockSpec((nb, tc, BR_W), lambda i, t: (i, t, 0))
    st = pl.BlockSpec((nb, RWKV_PAIRS, RWKV_HEAD_DIM, LANES), lambda i, t: (i, 0, 0, 0))
    rows = nb * RWKV_PAIRS * RWKV_HEAD_DIM
    y, sf = pl.pallas_call(
        functools.partial(_rwkv_scan_kernel, nb=nb, tc=tc),
        grid=(b // nb, t_len // tc),
        in_specs=[row] * 6 + [st, pl.BlockSpec((2 * LANES, LANES), lambda i, t: (0, 0))],
        out_specs=[pl.BlockSpec((nb, RWKV_PAIRS, tc, LANES), lambda i, t: (i, 0, t, 0)), st],
        out_shape=[jax.ShapeDtypeStruct((b, RWKV_PAIRS, t_len, LANES), F32),
                   jax.ShapeDtypeStruct((b, RWKV_PAIRS, RWKV_HEAD_DIM, LANES), F32)],
        scratch_shapes=[pltpu.VMEM((rows, LANES), F32), pltpu.VMEM((rows, LANES), F32),
                        pltpu.VMEM((rows, LANES), BF16)],
        compiler_params=_params(("parallel", "arbitrary")),
        name="rwkv_scan",
    )(r, d, k, v, a, bv, _pair_state(wkv_prev), _head_ones())
    return y, _unpair_state(sf)


def _rwkv_post_kernel(y_ref, bonus_ref, gate_ref, g_ref, b_ref, ones_ref, o_ref):
    ones2 = ones_ref[...]
    inv = 1.0 / RWKV_HEAD_DIM
    for p in range(RWKV_PAIRS):
        sl = slice(p * LANES, (p + 1) * LANES)
        y = y_ref[0, p]
        m = _seg_sum(y, ones2) * inv
        c = y - m
        var = _seg_sum(c * c, ones2) * inv
        out = c * lax.rsqrt(var + RWKV_LN_EPS) * g_ref[:, sl] + b_ref[:, sl] + bonus_ref[0, :, sl]
        o_ref[0, :, sl] = (out * _silu(gate_ref[0, :, sl])).astype(o_ref.dtype)


def _rwkv_post(y, bonus, p3d, lnx_g, lnx_b):
    b, t_len, _ = bonus.shape
    tt = min(t_len, 512)
    vec = pl.BlockSpec((1, BR_W), lambda i, t: (0, 0))
    return pl.pallas_call(
        _rwkv_post_kernel,
        grid=(b, t_len // tt),
        in_specs=[pl.BlockSpec((1, RWKV_PAIRS, tt, LANES), lambda i, t: (i, 0, t, 0)),
                  pl.BlockSpec((1, tt, BR_W), lambda i, t: (i, t, 0)),
                  _row_spec(tt, 'rwkv_gate'),
                  vec, vec,
                  pl.BlockSpec((2 * LANES, LANES), lambda i, t: (0, 0))],
        out_specs=pl.BlockSpec((1, tt, BR_W), lambda i, t: (i, t, 0)),
        out_shape=jax.ShapeDtypeStruct((b, t_len, BR_W), _branch_dtype(tt)),
        compiler_params=_params(("parallel", "parallel")),
        name="rwkv_post",
    )(y, bonus, p3d, lnx_g.reshape(1, BR_W), lnx_b.reshape(1, BR_W), _head_ones())


def _rope_tables(pos):
    def tab(half):
        freqs = ROPE_THETA ** (-jnp.arange(half, dtype=F32) / half)
        ang = pos.astype(F32)[:, None] * freqs[None, :]
        c, s = jnp.cos(ang), jnp.sin(ang)
        reps = LANES // (2 * half)
        return (jnp.tile(jnp.concatenate([c, c], axis=1), (1, reps)),
                jnp.tile(jnp.concatenate([-s, s], axis=1), (1, reps)))
    c128, s128 = tab(ATTN_HEAD_DIM // 2)
    c64, s64 = tab(IDX_DIM // 2)
    return c128, s128, c64, s64


def _rot128(x, cos, sin):
    return x * cos + pltpu.roll(x, ATTN_HEAD_DIM // 2, 1) * sin


def _rot64(x, cos, sin):
    half = IDX_DIM // 2
    lane = lax.broadcasted_iota(I32, x.shape, 1) % IDX_DIM
    partner = jnp.where(lane < half, pltpu.roll(x, LANES - half, 1), pltpu.roll(x, half, 1))
    return x * cos + partner * sin


def _attn_prep_kernel(q_ref, k_ref, v_ref, iq_ref, sm_ref, qn_ref, kn_ref, c128_ref, s128_ref, c64_ref, s64_ref,
                      qo_ref, kf_ref, kb_ref, vf_ref, vb_ref, iqo_ref, kif_ref, kia_ref, kib_ref, wo_ref):
    c128, s128, c64, s64 = c128_ref[...], s128_ref[...], c64_ref[...], s64_ref[...]
    scale = ATTN_HEAD_DIM ** -0.5
    lowp = qo_ref.dtype
    for h in range(ATTN_HEADS):
        sl = slice(h * LANES, (h + 1) * LANES)
        q = q_ref[0, :, sl]
        q = q * lax.rsqrt(jnp.mean(q * q, axis=-1, keepdims=True) + NORM_EPS) * qn_ref[...]
        qo_ref[0, :, sl] = (_rot128(q, c128, s128) * scale).astype(lowp)
        k = k_ref[0, :, sl]
        k = k * lax.rsqrt(jnp.mean(k * k, axis=-1, keepdims=True) + NORM_EPS) * kn_ref[...]
        k = _rot128(k, c128, s128)
        kf_ref[0, :, sl] = k
        kb_ref[0, :, sl] = k.astype(lowp)
        iqo_ref[0, :, sl] = _rot64(iq_ref[0, :, sl], c64, s64).astype(lowp)
    v = v_ref[0]
    vf_ref[0] = v
    vb_ref[0] = v.astype(lowp)
    sm = sm_ref[0, :, LANES:2 * LANES]
    ki = _rot64(sm, c64, s64)
    lane = lax.broadcasted_iota(I32, ki.shape, 1)
    kif_ref[0] = ki[:, 0:IDX_DIM]
    kz = jnp.where(lane < IDX_DIM, ki, 0.0)
    kia_ref[0] = kz.astype(lowp)
    kib_ref[0] = pltpu.roll(kz, IDX_DIM, 1).astype(lowp)
    wo_ref[0] = sm[:, IDX_DIM:IDX_DIM + IDX_HEADS] * IDX_SCALE


def _attn_prep(p3d, pos, qn, kn):
    b, t_len, _ = p3d.shape
    tt = min(t_len, 256)
    lowp = _branch_dtype(tt)
    c128, s128, c64, s64 = _rope_tables(pos)
    tab = pl.BlockSpec((tt, LANES), lambda i, t: (t, 0))
    hv = pl.BlockSpec((1, LANES), lambda i, t: (0, 0))
    o_row = pl.BlockSpec((1, tt, BR_W), lambda i, t: (i, t, 0))
    o_l = pl.BlockSpec((1, tt, LANES), lambda i, t: (i, t, 0))
    big_f = jax.ShapeDtypeStruct((b, t_len, BR_W), F32)
    big_b = jax.ShapeDtypeStruct((b, t_len, BR_W), lowp)
    outs = pl.pallas_call(
        _attn_prep_kernel,
        grid=(b, t_len // tt),
        in_specs=[_row_spec(tt, 'attn_q'), _row_spec(tt, 'attn_k'), _row_spec(tt, 'attn_v'), _row_spec(tt, 'idx_q'),
                  pl.BlockSpec((1, tt, SMALL_W), lambda i, t: (i, t, SMALL_BLK)),
                  hv, hv, tab, tab, tab, tab],
        out_specs=[o_row, o_row, o_row, o_row, o_row, o_row,
                   pl.BlockSpec((1, tt, IDX_DIM), lambda i, t: (i, t, 0)), o_l, o_l,
                   pl.BlockSpec((1, tt, IDX_HEADS), lambda i, t: (i, t, 0))],
        out_shape=[big_b, big_f, big_b, big_f, big_b, big_b,
                   jax.ShapeDtypeStruct((b, t_len, IDX_DIM), F32),
                   jax.ShapeDtypeStruct((b, t_len, LANES), lowp),
                   jax.ShapeDtypeStruct((b, t_len, LANES), lowp),
                   jax.ShapeDtypeStruct((b, t_len, IDX_HEADS), F32)],
        compiler_params=_params(("parallel", "parallel")),
        name="attn_prep",
    )(p3d, p3d, p3d, p3d, p3d, qn.reshape(1, LANES), kn.reshape(1, LANES), c128, s128, c64, s64)
    return tuple(o if o.dtype == F32 else o.astype(BF16) for o in outs)


def _sort_key(x):
    bits = pltpu.bitcast(x, I32)
    return jnp.where(bits < 0, bits ^ jnp.int32(0x7FFFFFFF), bits)


def _kth_largest_key(load_tile, n_tiles, rows, k_top):
    def count_tile(j, cand):
        x = load_tile(j)
        reps = x.shape[1] // LANES
        hit = jnp.where(x >= jnp.concatenate([cand] * reps, axis=1), 1.0, 0.0)
        part = hit[:, 0:LANES]
        for r in range(1, reps):
            part = part + hit[:, r * LANES:(r + 1) * LANES]
        return part

    def bit_step(i, lo):
        cand = lo + lax.shift_left(jnp.int32(1), jnp.int32(31) - i)
        if isinstance(n_tiles, int):
            acc = count_tile(0, cand)
            for j in range(1, n_tiles):
                acc = acc + count_tile(j, cand)
        else:
            acc = lax.fori_loop(0, n_tiles, lambda j, a: a + count_tile(j, cand),
                                jnp.zeros((rows, LANES), F32))
        cnt = jnp.broadcast_to(jnp.sum(acc, axis=-1, keepdims=True), (rows, LANES))
        return jnp.where(cnt >= k_top, cand, lo)

    lo = lax.fori_loop(0, 32, bit_step, jnp.full((rows, LANES), INT_MIN, I32))
    return jnp.maximum(lo, jnp.int32(INT_MIN + 1))


def _attn_prompt_kernel(q_ref, iq_ref, w_ref, gate_ref, k_ref, v_ref, kia_ref, kib_ref, o_ref,
                        key_scr, m_scr, l_scr, acc_scr, *, k_top, kc):
    qb = pl.program_id(1)
    n_chunks = ((qb + 1) * Q_BLOCK + kc - 1) // kc
    qpos = qb * Q_BLOCK + lax.broadcasted_iota(I32, (Q_BLOCK, kc), 0)
    w = w_ref[0]

    def score_chunk(c, carry):
        k0 = pl.multiple_of(c * kc, kc)
        ka = kia_ref[0, pl.ds(k0, kc), :]
        kb = kib_ref[0, pl.ds(k0, kc), :]
        sc = jnp.zeros((Q_BLOCK, kc), F32)
        for j in range(IDX_HEADS // 2):
            qp = iq_ref[0, :, j * LANES:(j + 1) * LANES]
            for hh, kk in ((0, ka), (1, kb)):
                rel = lax.dot_general(qp, kk, (((1,), (1,)), ((), ())), preferred_element_type=F32)
                h = 2 * j + hh
                sc = sc + jnp.maximum(rel, 0.0) * w[:, h:h + 1]
        spos = k0 + lax.broadcasted_iota(I32, (Q_BLOCK, kc), 1)
        key_scr[:, pl.ds(k0, kc)] = jnp.where(spos <= qpos, _sort_key(sc), jnp.int32(INT_MIN))
        return carry

    lax.fori_loop(0, n_chunks, score_chunk, 0)

    def load_tile(j):
        return key_scr[:, pl.ds(pl.multiple_of(j * kc, kc), kc)]

    reps = kc // LANES
    thr = _kth_largest_key(load_tile, n_chunks, Q_BLOCK, k_top)
    thr_c = jnp.concatenate([thr] * reps, axis=1)

    m_scr[...] = jnp.full(m_scr.shape, NEG_BIG, F32)
    l_scr[...] = jnp.zeros(l_scr.shape, F32)
    acc_scr[...] = jnp.zeros(acc_scr.shape, F32)

    def attend_chunk(c, carry):
        k0 = pl.multiple_of(c * kc, kc)
        sel = key_scr[:, pl.ds(k0, kc)] >= thr_c
        for h in range(ATTN_HEADS):
            sl = slice(h * LANES, (h + 1) * LANES)
            kh = k_ref[0, pl.ds(k0, kc), sl]
            vh = v_ref[0, pl.ds(k0, kc), sl]
            s = lax.dot_general(q_ref[0, :, sl], kh, (((1,), (1,)), ((), ())), preferred_element_type=F32)
            s = jnp.where(sel, s, NEG_BIG)
            m = m_scr[:, sl]
            m_new = jnp.maximum(m, jnp.broadcast_to(jnp.max(s, axis=-1, keepdims=True), (Q_BLOCK, LANES)))
            alpha = jnp.exp(m - m_new)
            p = jnp.exp(s - jnp.concatenate([m_new] * reps, axis=1))
            l_scr[:, sl] = l_scr[:, sl] * alpha + jnp.broadcast_to(jnp.sum(p, axis=-1, keepdims=True),
                                                                    (Q_BLOCK, LANES))
            acc_scr[:, sl] = acc_scr[:, sl] * alpha + jnp.dot(p.astype(BF16), vh, preferred_element_type=F32)
            m_scr[:, sl] = m_new
        return carry

    lax.fori_loop(0, n_chunks, attend_chunk, 0)
    o_ref[0] = (acc_scr[...] / l_scr[...] * _silu(gate_ref[0])).astype(BF16)


def _attn_prompt(prep, p3d):
    qb16, _, kb16, _, vb16, iqb16, _, kia, kib, w = prep
    b, t_len, _ = qb16.shape
    k_top = min(TOPK_MAX, t_len // 4)
    kc = min(256, t_len)
    assert t_len % kc == 0 and kc % Q_BLOCK == 0
    qrow = pl.BlockSpec((1, Q_BLOCK, BR_W), lambda i, t: (i, t, 0))

    def full(width):
        return pl.BlockSpec((1, t_len, width), lambda i, t: (i, 0, 0))

    return pl.pallas_call(
        functools.partial(_attn_prompt_kernel, k_top=k_top, kc=kc),
        grid=(b, t_len // Q_BLOCK),
        in_specs=[qrow, qrow,
                  pl.BlockSpec((1, Q_BLOCK, IDX_HEADS), lambda i, t: (i, t, 0)),
                  _row_spec(Q_BLOCK, 'attn_gate'),
                  full(BR_W), full(BR_W), full(LANES), full(LANES)],
        out_specs=pl.BlockSpec((1, Q_BLOCK, BR_W), lambda i, t: (i, t, 0)),
        out_shape=jax.ShapeDtypeStruct((b, t_len, BR_W), BF16),
        scratch_shapes=[pltpu.VMEM((Q_BLOCK, t_len), I32)] + [pltpu.VMEM((Q_BLOCK, BR_W), F32)] * 3,
        compiler_params=_params(("parallel", "arbitrary")),
        name="attn_prompt",
    )(qb16, iqb16, w, p3d, kb16, vb16, kia, kib)


SCORE_PAGES = 8
ATTEND_PAGES = 16


def _score_tile(qs, w, ki, t_new, new_rows):
    rel = lax.dot_general(qs, ki, (((1,), (1,)), ((), ())), preferred_element_type=F32)
    wr = jnp.maximum(rel, 0.0) * w
    sc = jnp.zeros((SUBLANES, PAGE_SIZE), F32)
    for t in range(t_new):
        sc = _put_row(sc, jnp.sum(wr[t * IDX_HEADS:(t + 1) * IDX_HEADS], axis=0, keepdims=True), t)
    qi = lax.broadcasted_iota(I32, sc.shape, 0)
    si = lax.broadcasted_iota(I32, sc.shape, 1)
    visible = qi < t_new
    if new_rows:
        visible = jnp.logical_and(visible, si <= qi)
    return jnp.where(visible, _sort_key(sc), jnp.int32(INT_MIN))


def _sample_scores_kernel(pt_ref, qs_ref, w_ref, *rest, t_new):
    pages = rest[:SCORE_PAGES]
    new_ref, o_ref, onew_ref = rest[SCORE_PAGES:]
    qs = qs_ref[0]
    w = w_ref[0]
    for j, page_ref in enumerate(pages):
        o_ref[0, j] = _score_tile(qs, w, page_ref[0, 0].astype(BF16), t_new, False)

    @pl.when(pl.program_id(1) == pl.num_programs(1) - 1)
    def _():
        onew_ref[0] = _score_tile(qs, w, new_ref[0], t_new, True)


def _sample_thresh_kernel(key_ref, knew_ref, o_ref, *, n_pages, k_top):
    def load_tile(j):
        return knew_ref[0] if j == n_pages else key_ref[0, j]

    o_ref[0] = _kth_largest_key(load_tile, n_pages + 1, SUBLANES, k_top)


def _sample_attend_kernel(pt_ref, q_ref, key_ref, knew_ref, thr_ref, gate_ref, exp_ref, *rest, t_new):
    kps = rest[:ATTEND_PAGES]
    vps = rest[ATTEND_PAGES:2 * ATTEND_PAGES]
    kn_ref, vn_ref, o_ref, m_scr, l_scr, acc_scr = rest[2 * ATTEND_PAGES:]
    p = pl.program_id(1)
    rows = t_new * ATTN_HEADS
    cols = PAGE_SIZE * ATTN_HEADS

    @pl.when(p == 0)
    def _():
        m_scr[...] = jnp.full(m_scr.shape, NEG_BIG, F32)
        l_scr[...] = jnp.zeros(l_scr.shape, F32)
        acc_scr[...] = jnp.zeros(acc_scr.shape, F32)

    q = q_ref[0]
    thr = thr_ref[0]

    def update(keys_list, kp_list, vp_list):
        n = len(keys_list)
        kp = jnp.concatenate(kp_list, axis=0) if n > 1 else kp_list[0]
        vp = jnp.concatenate(vp_list, axis=0) if n > 1 else vp_list[0]
        s = lax.dot_general(q, kp, (((1,), (1,)), ((), ())), preferred_element_type=F32)
        sels = []
        for keys in keys_list:
            sel = jnp.where(keys >= thr, 1.0, 0.0)
            sel = jnp.concatenate([jnp.broadcast_to(sel[t:t + 1, :], (ATTN_HEADS, PAGE_SIZE))
                                   for t in range(t_new)], axis=0)
            sels.append(jnp.dot(sel.astype(BF16), exp_ref[...], preferred_element_type=F32))
        sel = jnp.concatenate(sels, axis=1) if n > 1 else sels[0]
        own_head = (lax.broadcasted_iota(I32, s.shape, 0) % ATTN_HEADS
                    == lax.broadcasted_iota(I32, s.shape, 1) % ATTN_HEADS)
        s = jnp.where(jnp.logical_and(sel > 0.5, own_head), s, NEG_BIG)
        m = m_scr[...]
        m_new = jnp.maximum(m, jnp.max(s, axis=-1, keepdims=True))
        alpha = jnp.exp(m - m_new)
        pr = jnp.exp(s - m_new)
        l_scr[...] = l_scr[...] * alpha + jnp.sum(pr, axis=-1, keepdims=True)
        acc_scr[...] = acc_scr[...] * alpha + jnp.dot(pr.astype(BF16), vp, preferred_element_type=F32)
        m_scr[...] = m_new

    update([key_ref[0, j] for j in range(ATTEND_PAGES)],
           [kps[j][0, 0].astype(BF16) for j in range(ATTEND_PAGES)],
           [vps[j][0, 0].astype(BF16) for j in range(ATTEND_PAGES)])

    @pl.when(p == pl.num_programs(1) - 1)
    def _():
        update([knew_ref[0]], [kn_ref[0]], [vn_ref[0]])
        o_ref[0] = acc_scr[...] / l_scr[...] * _silu(gate_ref[0])


def _attn_sample(prep, p3d, cache_k, cache_v, cache_kidx, page_table, layer):
    qb16, _, kb16, _, vb16, iqb16, kif, _, _, w = prep
    b, t_new, _ = qb16.shape
    assert t_new <= SUBLANES
    n_pages = page_table.shape[1]
    assert n_pages % SCORE_PAGES == 0 and n_pages % ATTEND_PAGES == 0
    past = n_pages * PAGE_SIZE
    k_top = min(TOPK_MAX, (past + t_new) // 4)
    n_pool = cache_k.shape[0]
    cols = PAGE_SIZE * ATTN_HEADS
    rows = t_new * ATTN_HEADS
    ck = cache_k.reshape(n_pool, DEPTH, cols, ATTN_HEAD_DIM)
    cv = cache_v.reshape(n_pool, DEPTH, cols, ATTN_HEAD_DIM)
    n_qrow = t_new * IDX_HEADS
    qs = iqb16.reshape(b, n_qrow, IDX_DIM)
    w128 = jnp.broadcast_to(w.reshape(b, n_qrow, 1), (b, n_qrow, PAGE_SIZE))
    ki_new = jnp.pad(kif, ((0, 0), (0, PAGE_SIZE - t_new), (0, 0))).astype(BF16)

    def page(j, per_step):
        return lambda i, p, pt: (pt[i, p * per_step + j], layer, 0, 0)

    keys, keys_new = pl.pallas_call(
        functools.partial(_sample_scores_kernel, t_new=t_new),
        grid_spec=pltpu.PrefetchScalarGridSpec(
            num_scalar_prefetch=1,
            grid=(b, n_pages // SCORE_PAGES),
            in_specs=[pl.BlockSpec((1, n_qrow, IDX_DIM), lambda i, p, pt: (i, 0, 0)),
                      pl.BlockSpec((1, n_qrow, PAGE_SIZE), lambda i, p, pt: (i, 0, 0))]
                     + [pl.BlockSpec((1, 1, PAGE_SIZE, IDX_DIM), page(j, SCORE_PAGES)) for j in range(SCORE_PAGES)]
                     + [pl.BlockSpec((1, PAGE_SIZE, IDX_DIM), lambda i, p, pt: (i, 0, 0))],
            out_specs=[pl.BlockSpec((1, SCORE_PAGES, SUBLANES, PAGE_SIZE), lambda i, p, pt: (i, p, 0, 0)),
                       pl.BlockSpec((1, SUBLANES, PAGE_SIZE), lambda i, p, pt: (i, 0, 0))]),
        out_shape=[jax.ShapeDtypeStruct((b, n_pages, SUBLANES, PAGE_SIZE), I32),
                   jax.ShapeDtypeStruct((b, SUBLANES, PAGE_SIZE), I32)],
        compiler_params=_params(("parallel", "arbitrary")),
        name="sample_scores",
    )(page_table, qs, w128, *([cache_kidx] * SCORE_PAGES), ki_new)

    thr = pl.pallas_call(
        functools.partial(_sample_thresh_kernel, n_pages=n_pages, k_top=k_top),
        grid=(b,),
        in_specs=[pl.BlockSpec((1, n_pages, SUBLANES, PAGE_SIZE), lambda i: (i, 0, 0, 0)),
                  pl.BlockSpec((1, SUBLANES, PAGE_SIZE), lambda i: (i, 0, 0))],
        out_specs=pl.BlockSpec((1, SUBLANES, LANES), lambda i: (i, 0, 0)),
        out_shape=jax.ShapeDtypeStruct((b, SUBLANES, LANES), I32),
        compiler_params=_params(("parallel",)),
        name="sample_thresh",
    )(keys, keys_new)

    expand = jnp.asarray(np.arange(PAGE_SIZE)[:, None] == np.arange(cols)[None, :] // ATTN_HEADS, dtype=BF16)
    pad_new = lambda x: jnp.pad(x.reshape(b, rows, ATTN_HEAD_DIM), ((0, 0), (0, cols - rows), (0, 0)))
    gate = _seg(p3d, 'attn_gate').reshape(b, rows, ATTN_HEAD_DIM)
    whole = lambda r, c: pl.BlockSpec((1, r, c), lambda i, p, pt: (i, 0, 0))
    kv_page = lambda j: pl.BlockSpec((1, 1, cols, ATTN_HEAD_DIM), page(j, ATTEND_PAGES))
    out = pl.pallas_call(
        functools.partial(_sample_attend_kernel, t_new=t_new),
        grid_spec=pltpu.PrefetchScalarGridSpec(
            num_scalar_prefetch=1,
            grid=(b, n_pages // ATTEND_PAGES),
            in_specs=[whole(rows, ATTN_HEAD_DIM),
                      pl.BlockSpec((1, ATTEND_PAGES, SUBLANES, PAGE_SIZE), lambda i, p, pt: (i, p, 0, 0)),
                      whole(SUBLANES, PAGE_SIZE), whole(SUBLANES, LANES), whole(rows, ATTN_HEAD_DIM),
                      pl.BlockSpec((PAGE_SIZE, cols), lambda i, p, pt: (0, 0))]
                     + [kv_page(j) for j in range(ATTEND_PAGES)] + [kv_page(j) for j in range(ATTEND_PAGES)]
                     + [whole(cols, ATTN_HEAD_DIM), whole(cols, ATTN_HEAD_DIM)],
            out_specs=whole(rows, ATTN_HEAD_DIM),
            scratch_shapes=[pltpu.VMEM((rows, 1), F32), pltpu.VMEM((rows, 1), F32),
                            pltpu.VMEM((rows, ATTN_HEAD_DIM), F32)]),
        out_shape=jax.ShapeDtypeStruct((b, rows, ATTN_HEAD_DIM), F32),
        compiler_params=_params(("parallel", "arbitrary")),
        name="sample_attend",
    )(page_table, qb16.reshape(b, rows, ATTN_HEAD_DIM), keys, keys_new, thr, gate, expand,
      *([ck] * ATTEND_PAGES), *([cv] * ATTEND_PAGES), pad_new(kb16), pad_new(vb16))
    return out.reshape(b, t_new, BR_W)


def _layer(x, lw, pos0, pool_prev, shift_prev, wkv_prev, paged):
    b, t_len, _ = x.shape
    x2d = x.reshape(b * t_len, D_MODEL)
    pos = pos0 + jnp.arange(t_len, dtype=I32)
    p2d = _inproj(x2d, lw['norm_g'], lw['w_in'])
    p3d = p2d.reshape(b, t_len, IN_COLS_PAD)

    pool_br, pool_state = _pool(p3d, pool_prev, lw['pool_w'], lw['pool_scale'], pos0)
    gmlp_br, gmlp_vn = _gmlp(p3d, lw['gmlp_ln_g'], lw['gmlp_ln_b'], lw['gmlp_wm'], lw['gmlp_bsb'],
                             emit_vn=paged is not None)
    seqs, shift_state = _rwkv_prep(p3d, shift_prev, lw)
    y, wkv_state = _rwkv_scan(seqs[:6], wkv_prev)
    rwkv_br = _rwkv_post(y, seqs[6], p3d, lw['rwkv_lnx_g'], lw['rwkv_lnx_b'])
    prep = _attn_prep(p3d, pos, lw['attn_qn'], lw['attn_kn'])
    if paged is None:
        attn_br = _attn_prompt(prep, p3d)
    else:
        attn_br = _attn_sample(prep, p3d, *paged)

    flat = lambda t: t.reshape(b * t_len, BR_W).astype(BF16)
    merged = _merge([flat(pool_br), flat(gmlp_br), flat(rwkv_br), flat(attn_br)], lw['w_branch'], p2d)
    y_out = _outproj(x2d, merged, lw['w_out']).reshape(b, t_len, D_MODEL)
    hd = lambda t: t.reshape(b, t_len, ATTN_HEADS, ATTN_HEAD_DIM)
    return y_out, (hd(prep[1]), hd(prep[3]), prep[6], pool_state, shift_state, wkv_state, gmlp_vn)


def _lora_weights(w2, a2):
    z = jnp.zeros_like(w2)
    return jnp.concatenate([w2, z], axis=0).astype(BF16), jnp.concatenate([z, a2], axis=0).astype(BF16)


def kernel(x_prompt, x_sample, cache_k, cache_v, cache_kidx, page_table, state_pool, state_shift, state_wkv,
           norm_g, w_in, pool_w, pool_scale, gmlp_ln_g, gmlp_ln_b, gmlp_ws, gmlp_bs, rwkv_mu, rwkv_w0, rwkv_w2,
           rwkv_a0, rwkv_a2, rwkv_kk, rwkv_ka, rwkv_rk, rwkv_lnx_g, rwkv_lnx_b, attn_qn, attn_kn, w_branch, w_out):
    past = page_table.shape[1] * PAGE_SIZE
    bp = x_prompt.shape[0]
    w_in_p = _permute_w_in(w_in)
    w_branch_b = w_branch.astype(BF16)
    w_out_b = w_out.astype(BF16)
    pool_w_b = pool_w.astype(BF16)
    causal = jnp.tril(jnp.ones((GMLP_CHUNK, GMLP_CHUNK), dtype=bool))
    gmlp_wm = jnp.where(causal[None, None], gmlp_ws, 0.0).astype(BF16)
    gmlp_bsb = jnp.broadcast_to(gmlp_bs[..., None], gmlp_bs.shape + (GMLP_GROUP_W,))
    xp, xs = x_prompt, x_sample
    outs_p, outs_s = [], []
    for l in range(DEPTH):
        wl, al = _lora_weights(rwkv_w2[l], rwkv_a2[l])
        lw = {
            'norm_g': norm_g[l], 'w_in': w_in_p[l], 'pool_w': pool_w_b[l], 'pool_scale': pool_scale[l],
            'gmlp_ln_g': gmlp_ln_g[l], 'gmlp_ln_b': gmlp_ln_b[l], 'gmlp_wm': gmlp_wm[l], 'gmlp_bsb': gmlp_bsb[l],
            'rwkv_mu': rwkv_mu[l], 'rwkv_w0': rwkv_w0[l], 'rwkv_wl': wl, 'rwkv_a0': rwkv_a0[l],
            'rwkv_al': al, 'rwkv_kk': rwkv_kk[l], 'rwkv_ka': rwkv_ka[l], 'rwkv_rk': rwkv_rk[l].reshape(BR_W),
            'rwkv_lnx_g': rwkv_lnx_g[l], 'rwkv_lnx_b': rwkv_lnx_b[l], 'attn_qn': attn_qn[l],
            'attn_kn': attn_kn[l], 'w_branch': w_branch_b[l], 'w_out': w_out_b[l],
        }
        xp, st_p = _layer(
            xp, lw, 0,
            jnp.zeros((bp, POOL_STATE, BR_W), F32),
            jnp.zeros((bp, RWKV_SHIFT_W), F32),
            jnp.zeros((bp, RWKV_HEADS, RWKV_HEAD_DIM, RWKV_HEAD_DIM), F32),
            None)
        xs, st_s = _layer(xs, lw, past, state_pool[:, l], state_shift[:, l], state_wkv[:, l],
                          (cache_k, cache_v, cache_kidx, page_table, l))
        outs_p.append(st_p)
        outs_s.append(st_s)
    stk = lambda outs, i: jnp.stack([o[i] for o in outs], axis=1)
    k_p, v_p, ki_p = stk(outs_p, 0), stk(outs_p, 1), stk(outs_p, 2)
    pool_p, shift_p, wkv_p = stk(outs_p, 3), stk(outs_p, 4), stk(outs_p, 5)
    k_s, v_s, ki_s = stk(outs_s, 0), stk(outs_s, 1), stk(outs_s, 2)
    pool_s, shift_s, wkv_s = stk(outs_s, 3), stk(outs_s, 4), stk(outs_s, 5)
    gmlp_v_s = stk(outs_s, 6)
    return (xp, xs, k_p, v_p, ki_p, k_s, v_s, ki_s, pool_p, pool_s, shift_p, shift_s, wkv_p, wkv_s, gmlp_v_s)
```
